```python
import jax, jax.numpy as jnp
from jax import lax
import numpy as np

D_MODEL = 1024
BATCH = 4
SEQ = 4096
DEPTH = 1

CHUNK = 64
D_MIX = D_MODEL
DN_HEAD_DIM = 128
DN_WIDTH = D_MIX // 2
DN_HEADS = DN_WIDTH // DN_HEAD_DIM
CONV_K = 4
SG_WIDTH = D_MIX - DN_WIDTH
SG_GROUPS = 4
SG_DIM = SG_WIDTH // SG_GROUPS
SG_BLOCK = 128
D_FF = 2816
FFN_CONV = 3
EPS = 1e-6
PROJ_COLS = 4 * DN_WIDTH + 2 * SG_WIDTH + 2 * DN_HEADS

kernel_name = "hybrid_gdn_gmlp_convffn_block"


def rmsnorm(x, g):
    xf = x.astype(jnp.float32)
    y = xf * lax.rsqrt(jnp.mean(xf * xf, axis=-1, keepdims=True) + EPS)
    return (y * g.astype(jnp.float32)).astype(x.dtype)


def l2norm(x):
    return x * lax.rsqrt(jnp.sum(x * x, axis=-1, keepdims=True) + EPS)


def causal_dwconv(x, w):
    K = w.shape[0]
    T = x.shape[1]
    xp = jnp.pad(x, ((0, 0), (K - 1, 0), (0, 0)))
    out = xp[:, 0:T] * w[0]
    for k in range(1, K):
        out = out + xp[:, k:k + T] * w[k]
    return out


def gated_delta_chunked(q, k, v, g, beta):
    B, T, H, D = q.shape
    N = T // CHUNK
    C = CHUNK
    ch = lambda a: a.reshape(B, N, C, H, D).transpose(0, 3, 1, 2, 4)
    q, k, v = ch(q), ch(k), ch(v)
    g = g.reshape(B, N, C, H).transpose(0, 3, 1, 2)
    beta = beta.reshape(B, N, C, H).transpose(0, 3, 1, 2)
    Gc = jnp.cumsum(g, axis=-1)
    incl = jnp.tril(jnp.ones((C, C), dtype=bool))
    strict = jnp.tril(jnp.ones((C, C), dtype=bool), -1)
    decay = jnp.exp(jnp.where(incl, Gc[..., :, None] - Gc[..., None, :], -jnp.inf))
    k_beta = k * beta[..., None]
    v_beta = v * beta[..., None]
    L = jnp.where(strict, jnp.einsum('bhnid,bhnjd->bhnij', k_beta, k) * decay, 0.0)
    eye = jnp.eye(C, dtype=jnp.float32)
    rhs = jnp.concatenate([v_beta, k_beta * jnp.exp(Gc)[..., None]], axis=-1)
    sol = lax.linalg.triangular_solve(eye + L, rhs, left_side=True, lower=True,
                                      transpose_a=False, conjugate_a=False, unit_diagonal=True)
    value, k_cumdecay = sol[..., :D], sol[..., D:]
    attn_intra = jnp.einsum('bhnid,bhnjd->bhnij', q, k) * decay
    q_decay = q * jnp.exp(Gc)[..., None]
    k_tail = k * jnp.exp(Gc[..., -1:] - Gc)[..., None]
    chunk_decay = jnp.exp(Gc[..., -1])

    def step(S, inp):
        a_i, val_i, kcd_i, qd_i, kt_i, cd_i = inp
        v_new = val_i - jnp.einsum('bhcd,bhde->bhce', kcd_i, S)
        o = jnp.einsum('bhcd,bhde->bhce', qd_i, S) + jnp.einsum('bhij,bhje->bhie', a_i, v_new)
        S = S * cd_i[..., None, None] + jnp.einsum('bhcd,bhce->bhde', kt_i, v_new)
        return S, o

    mv = lambda a: jnp.moveaxis(a, 2, 0)
    S0 = jnp.zeros((B, H, D, D), jnp.float32)
    _, o = lax.scan(step, S0, (mv(attn_intra), mv(value), mv(k_cumdecay), mv(q_decay),
                               mv(k_tail), mv(chunk_decay)))
    return o.transpose(1, 0, 3, 2, 4).reshape(B, T, H, D)


def deltanet_group(q_raw, k_raw, v_raw, gate, b_raw, a_raw, conv_w, a_log, dt_bias, norm_g):
    B, T, _ = q_raw.shape
    qkv = jax.nn.silu(causal_dwconv(jnp.concatenate([q_raw, k_raw, v_raw], axis=-1), conv_w))
    qkv = qkv.astype(jnp.float32).reshape(B, T, 3, DN_HEADS, DN_HEAD_DIM)
    q = l2norm(qkv[:, :, 0]) * (DN_HEAD_DIM ** -0.5)
    k = l2norm(qkv[:, :, 1])
    v = qkv[:, :, 2]
    beta = jax.nn.sigmoid(b_raw.astype(jnp.float32))
    g = -jnp.exp(a_log.astype(jnp.float32)) * jax.nn.softplus(a_raw.astype(jnp.float32) + dt_bias.astype(jnp.float32))
    o = gated_delta_chunked(q, k, v, g, beta)
    o = rmsnorm(o, norm_g).reshape(B, T, DN_WIDTH)
    return (o * jax.nn.silu(gate.astype(jnp.float32))).astype(q_raw.dtype)


def gmlp_group(u_raw, v_raw, norm_g, w_s, b_s):
    B, T, _ = u_raw.shape
    u = jax.nn.gelu(u_raw)
    v = jax.nn.gelu(v_raw).reshape(B, T, SG_GROUPS, SG_DIM)
    v = rmsnorm(v, norm_g.reshape(SG_GROUPS, SG_DIM))
    v = v.reshape(B, T // SG_BLOCK, SG_BLOCK, SG_GROUPS, SG_DIM)
    pos_chunk = jnp.arange(SG_BLOCK) // CHUNK
    mask = pos_chunk[None, :] <= pos_chunk[:, None]
    w_m = jnp.where(mask, w_s, 0.0).astype(v.dtype)
    s = jnp.einsum('gij,bnjgd->bnigd', w_m, v) + b_s.T[None, None, :, :, None]
    return u * s.reshape(B, T, SG_WIDTH)


def setup_inputs(seed: int = 0) -> dict:
    key = jax.random.key(seed)
    ks = jax.random.split(key, 20)
    nrm = lambda k, shape, s: jax.random.normal(k, shape, jnp.float32) * s
    dt = jnp.exp(jax.random.uniform(ks[5], (DEPTH, DN_HEADS), jnp.float32,
                                    np.log(1e-3), np.log(1e-1)))
    return {
        "x": nrm(ks[0], (BATCH, SEQ, D_MODEL), 1.0),
        "attn_norm_g": 1.0 + nrm(ks[1], (DEPTH, D_MODEL), 0.02),
        "w_in": nrm(ks[2], (DEPTH, D_MODEL, PROJ_COLS), D_MODEL ** -0.5),
        "dn_conv_w": nrm(ks[3], (DEPTH, CONV_K, 3 * DN_WIDTH), CONV_K ** -0.5),
        "dn_a_log": jnp.log(jax.random.uniform(ks[4], (DEPTH, DN_HEADS), jnp.float32, 1.0, 16.0)),
        "dn_dt_bias": dt + jnp.log(-jnp.expm1(-dt)),
        "dn_out_norm_g": 1.0 + nrm(ks[6], (DEPTH, DN_HEAD_DIM), 0.02),
        "sg_norm_g": 1.0 + nrm(ks[7], (DEPTH, SG_WIDTH), 0.02),
        "sg_w": nrm(ks[8], (DEPTH, SG_GROUPS, SG_BLOCK, SG_BLOCK), SG_BLOCK ** -0.5),
        "sg_b": 1.0 + nrm(ks[9], (DEPTH, SG_GROUPS, SG_BLOCK), 0.01),
        "w_out": nrm(ks[10], (DEPTH, D_MIX, D_MODEL), D_MIX ** -0.5),
        "ffn_norm_g": 1.0 + nrm(ks[11], (DEPTH, D_MODEL), 0.02),
        "w_up": nrm(ks[12], (DEPTH, D_MODEL, 2 * D_FF), D_MODEL ** -0.5),
        "ffn_conv_w": nrm(ks[13], (DEPTH, FFN_CONV, 2 * D_FF), FFN_CONV ** -0.5),
        "ffn_conv_b": nrm(ks[14], (DEPTH, 2 * D_FF), 0.01),
        "w_down": nrm(ks[15], (DEPTH, D_FF, D_MODEL), D_FF ** -0.5),
        "final_norm_g": 1.0 + nrm(ks[16], (D_MODEL,), 0.02),
    }


def reference(x, attn_norm_g, w_in, dn_conv_w, dn_a_log, dn_dt_bias, dn_out_norm_g,
              sg_norm_g, sg_w, sg_b, w_out, ffn_norm_g, w_up, ffn_conv_w, ffn_conv_b,
              w_down, final_norm_g):
    split_at = np.cumsum([DN_WIDTH] * 4 + [SG_WIDTH] * 2 + [DN_HEADS]).tolist()
    for l in range(DEPTH):
        h = rmsnorm(x, attn_norm_g[l])
        p = jnp.einsum('btd,dc->btc', h, w_in[l])
        q_raw, k_raw, v_raw, gate, u_raw, vg_raw, b_raw, a_raw = jnp.split(p, split_at, axis=-1)
        o_dn = deltanet_group(q_raw, k_raw, v_raw, gate, b_raw, a_raw, dn_conv_w[l],
                              dn_a_log[l], dn_dt_bias[l], dn_out_norm_g[l])
        o_sg = gmlp_group(u_raw, vg_raw, sg_norm_g[l], sg_w[l], sg_b[l])
        mix = jnp.concatenate([o_dn, o_sg], axis=-1)
        x = x + jnp.einsum('btc,cd->btd', mix, w_out[l])
        h = rmsnorm(x, ffn_norm_g[l])
        up = jnp.einsum('btd,df->btf', h, w_up[l])
        up = causal_dwconv(up, ffn_conv_w[l]) + ffn_conv_b[l]
        g_ff, v_ff = jnp.split(up, 2, axis=-1)
        x = x + jnp.einsum('btf,fd->btd', jax.nn.silu(g_ff) * v_ff, w_down[l])
    return rmsnorm(x, final_norm_g)
```

```python
import functools
import math

import jax
import jax.numpy as jnp
from jax import lax
from jax.experimental import pallas as pl
from jax.experimental.pallas import tpu as pltpu

F32 = jnp.float32
BF16 = jnp.bfloat16

EPS = 1e-6
CHUNK = 64
HEAD_DIM = 128
N_HEADS = 4
DN_WIDTH = N_HEADS * HEAD_DIM
SG_GROUPS = 4
SG_DIM = 128
SG_BLOCK = 128
SG_WIDTH = SG_GROUPS * SG_DIM
CONV_K = 4
FFN_CONV = 3
HALO = 8

V7X_VMEM_LIMIT_BYTES = 56 * 1024 * 1024

HIGHEST = lax.Precision.HIGHEST


def _silu(x):
    return x * jax.nn.sigmoid(x)


def _gelu_tanh(x):
    c = math.sqrt(2.0 / math.pi)
    return x * (0.5 * (1.0 + jnp.tanh(c * (x + 0.044715 * (x * x * x)))))


def _softplus(x):
    return jnp.maximum(x, 0.0) + jnp.log1p(jnp.exp(-jnp.abs(x)))


def _rms(x):
    return x * lax.rsqrt(jnp.mean(x * x, axis=-1, keepdims=True) + EPS)


def _dot(a, b, precision=None):
    return jnp.dot(a, b, preferred_element_type=F32, precision=precision)


def _dot_nt(a, b):
    return lax.dot_general(a, b, (((1,), (1,)), ((), ())), preferred_element_type=F32)


def _dot_tn(a, b):
    return lax.dot_general(a, b, (((0,), (0,)), ((), ())), preferred_element_type=F32)


def _inproj_kernel(x_ref, ng_ref, w_ref, wba_ref, cw_ref, alog_ref, dtb_ref, sgn_ref,
                   sgw_ref, sgb_ref,
                   q_ref, k_ref, v_ref, gate_ref, gb_ref, osg_ref,
                   pre_ref, *, tt):
    t = pl.program_id(1)
    h = (_rms(x_ref[0]) * ng_ref[...]).astype(BF16)

    @pl.when(t == 0)
    def _():
        pre_ref[0:HALO, :] = jnp.zeros((HALO, 3 * DN_WIDTH), F32)

    @pl.when(t > 0)
    def _():
        pre_ref[0:HALO, :] = pre_ref[tt:tt + HALO, :]

    pre_ref[HALO:HALO + tt, :] = _dot(h, w_ref[:, 0:3 * DN_WIDTH])
    conv = pre_ref[HALO - 3:HALO - 3 + tt, :] * cw_ref[0:1, :]
    for j in range(1, CONV_K):
        conv = conv + pre_ref[HALO - 3 + j:HALO - 3 + j + tt, :] * cw_ref[j:j + 1, :]
    qkv = _silu(conv)
    for hd in range(N_HEADS):
        sl = slice(hd * HEAD_DIM, (hd + 1) * HEAD_DIM)
        qh = qkv[:, sl]
        q_ref[0, :, sl] = qh * lax.rsqrt(jnp.sum(qh * qh, axis=-1, keepdims=True) + EPS) * (HEAD_DIM ** -0.5)
        kh = qkv[:, DN_WIDTH + hd * HEAD_DIM:DN_WIDTH + (hd + 1) * HEAD_DIM]
        k_ref[0, :, sl] = kh * lax.rsqrt(jnp.sum(kh * kh, axis=-1, keepdims=True) + EPS)
    v_ref[0] = qkv[:, 2 * DN_WIDTH:3 * DN_WIDTH]

    gate_ref[0] = _silu(_dot(h, w_ref[:, 3 * DN_WIDTH:4 * DN_WIDTH]))

    ba = _dot(h, wba_ref[...])
    lane = lax.broadcasted_iota(jnp.int32, ba.shape, 1)
    g = -jnp.exp(alog_ref[...]) * _softplus(ba + dtb_ref[...])
    gb_ref[0] = jnp.where(lane < N_HEADS, jax.nn.sigmoid(ba), g)

    u0 = 4 * DN_WIDTH
    u = _gelu_tanh(_dot(h, w_ref[:, u0:u0 + SG_WIDTH]))
    vg = _gelu_tanh(_dot(h, w_ref[:, u0 + SG_WIDTH:u0 + 2 * SG_WIDTH]))
    ri = lax.broadcasted_iota(jnp.int32, (SG_BLOCK, SG_BLOCK), 0) // CHUNK
    ci = lax.broadcasted_iota(jnp.int32, (SG_BLOCK, SG_BLOCK), 1) // CHUNK
    for gi in range(SG_GROUPS):
        sl = slice(gi * SG_DIM, (gi + 1) * SG_DIM)
        vn = (_rms(vg[:, sl]) * sgn_ref[:, sl]).astype(BF16)
        wm = jnp.where(ci <= ri, sgw_ref[gi], 0.0).astype(BF16)
        for nb in range(tt // SG_BLOCK):
            rs = slice(nb * SG_BLOCK, (nb + 1) * SG_BLOCK)
            s = _dot(wm, vn[rs]) + sgb_ref[gi]
            osg_ref[0, rs, sl] = (u[rs, sl] * s).astype(BF16)


def _unit_lower_inverse(lmat, eye):
    n = lmat.shape[0]
    p = eye - lmat
    m = _dot(lmat, lmat, HIGHEST)
    span = 2
    while True:
        p = p + _dot(p, m, HIGHEST)
        span *= 2
        if span >= n:
            return p
        m = _dot(m, m, HIGHEST)


def _deltanet_kernel(q_ref, k_ref, v_ref, gate_ref, gb_ref, og_ref, o_ref, s_ref, *, tt):
    t = pl.program_id(1)

    @pl.when(t == 0)
    def _():
        s_ref[...] = jnp.zeros(s_ref.shape, F32)

    c = CHUNK
    row = lax.broadcasted_iota(jnp.int32, (c, c), 0)
    col = lax.broadcasted_iota(jnp.int32, (c, c), 1)
    incl = row >= col
    strict = row > col
    tril = incl.astype(F32)
    eye = (row == col).astype(F32)

    def chunk_body(ci, carry):
        r0 = pl.multiple_of(ci * c, c)
        rows = pl.ds(r0, c)
        gb = gb_ref[0, rows, :]
        for hd in range(N_HEADS):
            sl = slice(hd * HEAD_DIM, (hd + 1) * HEAD_DIM)
            beta = jnp.broadcast_to(gb[:, hd:hd + 1], (c, HEAD_DIM))
            g = jnp.broadcast_to(gb[:, N_HEADS + hd:N_HEADS + hd + 1], (c, HEAD_DIM))
            gc = _dot(tril, g, HIGHEST)
            g_last = gc[c - 1:c, :]
            e_gc = jnp.exp(gc)
            e_tail = jnp.exp(g_last - gc)
            cd = jnp.exp(g_last)
            gc_cols = jnp.transpose(gc)[0:c, :]
            decay = jnp.exp(jnp.where(incl, gc[:, 0:c] - gc_cols, -jnp.inf))

            qh = q_ref[0, rows, sl]
            kh = k_ref[0, rows, sl]
            vh = v_ref[0, rows, sl]
            kb = kh * beta
            vb = vh * beta
            kq = _dot_nt(jnp.concatenate([kb, qh], axis=0).astype(BF16), kh.astype(BF16))
            lmat = jnp.where(strict, kq[0:c] * decay, 0.0)
            attn = kq[c:2 * c] * decay
            tinv = _unit_lower_inverse(lmat, eye)
            rhs = jnp.concatenate([vb, kb * e_gc], axis=1)
            sol = _dot(tinv, rhs, HIGHEST)
            value = sol[:, 0:HEAD_DIM]
            kcd = sol[:, HEAD_DIM:2 * HEAD_DIM]
            qd = qh * e_gc
            ktail = kh * e_tail

            s = s_ref[hd]
            ps = _dot(jnp.concatenate([kcd, qd], axis=0).astype(BF16), s.astype(BF16))
            v_new = value - ps[0:c]
            o = ps[c:2 * c] + _dot(attn.astype(BF16), v_new.astype(BF16))
            s_ref[hd] = s * cd + _dot_tn(ktail.astype(BF16), v_new.astype(BF16))

            o_ref[0, rows, sl] = (_rms(o) * og_ref[...] * gate_ref[0, rows, sl]).astype(BF16)
        return carry

    lax.fori_loop(0, tt // c, chunk_body, 0)


def _ffn_kernel(x_ref, odn_ref, osg_ref, wo_ref, fg_ref, wup_ref, cw_ref, cb_ref, wdn_ref, fin_ref,
                out_ref, up_ref, act_ref, *, tt, d_ff, col_tile, final):
    t = pl.program_id(1)
    x1 = (x_ref[0] + _dot(odn_ref[0], wo_ref[0:DN_WIDTH, :])
          + _dot(osg_ref[0], wo_ref[DN_WIDTH:DN_WIDTH + SG_WIDTH, :]))
    h = (_rms(x1) * fg_ref[...]).astype(BF16)

    @pl.when(t == 0)
    def _():
        up_ref[0:HALO, :] = jnp.zeros((HALO, 2 * d_ff), F32)

    @pl.when(t > 0)
    def _():
        up_ref[0:HALO, :] = up_ref[tt:tt + HALO, :]

    for j in range(2 * d_ff // col_tile):
        cs = slice(j * col_tile, (j + 1) * col_tile)
        up_ref[HALO:HALO + tt, cs] = _dot(h, wup_ref[:, cs])

    def conv(cs):
        acc = up_ref[HALO - 2:HALO - 2 + tt, cs] * cw_ref[0:1, cs] + cb_ref[:, cs]
        for j in range(1, FFN_CONV):
            acc = acc + up_ref[HALO - 2 + j:HALO - 2 + j + tt, cs] * cw_ref[j:j + 1, cs]
        return acc

    for j in range(d_ff // col_tile):
        gs = slice(j * col_tile, (j + 1) * col_tile)
        vs = slice(d_ff + j * col_tile, d_ff + (j + 1) * col_tile)
        act_ref[:, gs] = (_silu(conv(gs)) * conv(vs)).astype(BF16)

    x2 = x1 + _dot(act_ref[...], wdn_ref[...])
    out_ref[0] = _rms(x2) * fin_ref[...] if final else x2


def _const_spec(shape):
    nd = len(shape)
    return pl.BlockSpec(shape, lambda b, t: (0,) * nd, pipeline_mode=pl.Buffered(1))


def _tile_spec(tt, width):
    return pl.BlockSpec((1, tt, width), lambda b, t: (b, t, 0))


def kernel(x, attn_norm_g, w_in, dn_conv_w, dn_a_log, dn_dt_bias, dn_out_norm_g, sg_norm_g, sg_w, sg_b,
           w_out, ffn_norm_g, w_up, ffn_conv_w, ffn_conv_b, w_down, final_norm_g):
    bsz, seq, d_model = x.shape
    depth = w_in.shape[0]
    d_ff = w_down.shape[1]
    n_main = 4 * DN_WIDTH + 2 * SG_WIDTH
    assert w_in.shape[2] == n_main + 2 * N_HEADS
    params = pltpu.CompilerParams(dimension_semantics=("arbitrary", "arbitrary"),
                                  vmem_limit_bytes=V7X_VMEM_LIMIT_BYTES)

    tt1, tt2, tt3 = 256, 512, 256
    col_tile = 128 * math.gcd(d_ff // 128, 11)
    assert seq % tt1 == 0 and seq % tt2 == 0 and seq % tt3 == 0 and d_ff % col_tile == 0

    for l in range(depth):
        w_main = w_in[l, :, :n_main].astype(BF16)
        w_ba = jnp.pad(w_in[l, :, n_main:], ((0, 0), (0, 128 - 2 * N_HEADS))).astype(BF16)
        lane_pad = (N_HEADS, 128 - 2 * N_HEADS)
        alog_row = jnp.pad(dn_a_log[l].astype(F32), lane_pad).reshape(1, 128)
        dtb_row = jnp.pad(dn_dt_bias[l].astype(F32), lane_pad).reshape(1, 128)
        sgb_rep = jnp.broadcast_to(sg_b[l].astype(F32)[:, :, None], (SG_GROUPS, SG_BLOCK, SG_DIM))

        q, k, v, gate, gb, o_sg = pl.pallas_call(
            functools.partial(_inproj_kernel, tt=tt1),
            grid=(bsz, seq // tt1),
            in_specs=[
                _tile_spec(tt1, d_model),
                _const_spec((1, d_model)),
                _const_spec((d_model, n_main)),
                _const_spec((d_model, 128)),
                _const_spec((CONV_K, 3 * DN_WIDTH)),
                _const_spec((1, 128)),
                _const_spec((1, 128)),
                _const_spec((1, SG_WIDTH)),
                _const_spec((SG_GROUPS, SG_BLOCK, SG_BLOCK)),
                _const_spec((SG_GROUPS, SG_BLOCK, SG_DIM)),
            ],
            out_specs=[
                _tile_spec(tt1, DN_WIDTH), _tile_spec(tt1, DN_WIDTH), _tile_spec(tt1, DN_WIDTH),
                _tile_spec(tt1, DN_WIDTH), _tile_spec(tt1, 128), _tile_spec(tt1, SG_WIDTH),
            ],
            out_shape=[
                jax.ShapeDtypeStruct((bsz, seq, DN_WIDTH), F32),
                jax.ShapeDtypeStruct((bsz, seq, DN_WIDTH), F32),
                jax.ShapeDtypeStruct((bsz, seq, DN_WIDTH), F32),
                jax.ShapeDtypeStruct((bsz, seq, DN_WIDTH), F32),
                jax.ShapeDtypeStruct((bsz, seq, 128), F32),
                jax.ShapeDtypeStruct((bsz, seq, SG_WIDTH), BF16),
            ],
            scratch_shapes=[pltpu.VMEM((tt1 + HALO, 3 * DN_WIDTH), F32)],
            compiler_params=params,
            name="inproj",
        )(x, attn_norm_g[l].reshape(1, d_model), w_main, w_ba, dn_conv_w[l], alog_row, dtb_row,
          sg_norm_g[l].reshape(1, SG_WIDTH), sg_w[l], sgb_rep)

        o_dn = pl.pallas_call(
            functools.partial(_deltanet_kernel, tt=tt2),
            grid=(bsz, seq // tt2),
            in_specs=[
                _tile_spec(tt2, DN_WIDTH), _tile_spec(tt2, DN_WIDTH), _tile_spec(tt2, DN_WIDTH),
                _tile_spec(tt2, DN_WIDTH), _tile_spec(tt2, 128),
                _const_spec((1, HEAD_DIM)),
            ],
            out_specs=_tile_spec(tt2, DN_WIDTH),
            out_shape=jax.ShapeDtypeStruct((bsz, seq, DN_WIDTH), BF16),
            scratch_shapes=[pltpu.VMEM((N_HEADS, HEAD_DIM, HEAD_DIM), F32)],
            compiler_params=params,
            name="deltanet",
        )(q, k, v, gate, gb, dn_out_norm_g[l].reshape(1, HEAD_DIM))

        x = pl.pallas_call(
            functools.partial(_ffn_kernel, tt=tt3, d_ff=d_ff, col_tile=col_tile, final=(l == depth - 1)),
            grid=(bsz, seq // tt3),
            in_specs=[
                _tile_spec(tt3, d_model), _tile_spec(tt3, DN_WIDTH), _tile_spec(tt3, SG_WIDTH),
                _const_spec((DN_WIDTH + SG_WIDTH, d_model)),
                _const_spec((1, d_model)),
                _const_spec((d_model, 2 * d_ff)),
                _const_spec((FFN_CONV, 2 * d_ff)),
                _const_spec((1, 2 * d_ff)),
                _const_spec((d_ff, d_model)),
                _const_spec((1, d_model)),
            ],
            out_specs=_tile_spec(tt3, d_model),
            out_shape=jax.ShapeDtypeStruct((bsz, seq, d_model), F32),
            scratch_shapes=[pltpu.VMEM((tt3 + HALO, 2 * d_ff), F32),
                            pltpu.VMEM((tt3, d_ff), BF16)],
            compiler_params=params,
            name="ffn",
        )(x, o_dn, o_sg, w_out[l].astype(BF16), ffn_norm_g[l].reshape(1, d_model),
          w_up[l].astype(BF16), ffn_conv_w[l], ffn_conv_b[l].reshape(1, 2 * d_ff),
          w_down[l].astype(BF16),
          final_norm_g.reshape(1, d_model))
    return x
```

```python
import functools
import math

import jax
import jax.numpy as jnp
from jax import lax
from jax.experimental import pallas as pl
from jax.experimental.pallas import tpu as pltpu

F32 = jnp.float32
BF16 = jnp.bfloat16

EPS = 1e-6
CHUNK = 64
HEAD_DIM = 128
N_HEADS = 4
DN_WIDTH = N_HEADS * HEAD_DIM
SG_GROUPS = 4
SG_DIM = 128
SG_BLOCK = 128
SG_WIDTH = SG_GROUPS * SG_DIM
CONV_K = 4
FFN_CONV = 3
HALO = 8

V7X_VMEM_LIMIT_BYTES = 56 * 1024 * 1024

HIGHEST = lax.Precision.HIGHEST


def _silu(x):
    return x * jax.nn.sigmoid(x)


def _gelu_tanh(x):
    c = math.sqrt(2.0 / math.pi)
    return x * (0.5 * (1.0 + jnp.tanh(c * (x + 0.044715 * (x * x * x)))))


def _softplus(x):
    return jnp.maximum(x, 0.0) + jnp.log1p(jnp.exp(-jnp.abs(x)))


def _rms(x):
    return x * lax.rsqrt(jnp.mean(x * x, axis=-1, keepdims=True) + EPS)


def _dot(a, b, precision=None):
    return jnp.dot(a, b, preferred_element_type=F32, precision=precision)


def _dot_nt(a, b):
    return lax.dot_general(a, b, (((1,), (1,)), ((), ())), preferred_element_type=F32)


def _dot_tn(a, b):
    return lax.dot_general(a, b, (((0,), (0,)), ((), ())), preferred_element_type=F32)


def _inproj_kernel(x_ref, ng_ref, w_ref, wba_ref, cw_ref, alog_ref, dtb_ref, sgn_ref,
                   sgw_ref, sgb_ref,
                   q_ref, k_ref, v_ref, gate_ref, gb_ref, osg_ref,
                   pre_ref, *, tt):
    t = pl.program_id(1)
    h = (_rms(x_ref[0]) * ng_ref[...]).astype(BF16)

    @pl.when(t == 0)
    def _():
        pre_ref[0:HALO, :] = jnp.zeros((HALO, 3 * DN_WIDTH), F32)

    @pl.when(t > 0)
    def _():
        pre_ref[0:HALO, :] = pre_ref[tt:tt + HALO, :]

    pre_ref[HALO:HALO + tt, :] = _dot(h, w_ref[:, 0:3 * DN_WIDTH])
    conv = pre_ref[HALO - 3:HALO - 3 + tt, :] * cw_ref[0:1, :]
    for j in range(1, CONV_K):
        conv = conv + pre_ref[HALO - 3 + j:HALO - 3 + j + tt, :] * cw_ref[j:j + 1, :]
    qkv = _silu(conv)
    for hd in range(N_HEADS):
        sl = slice(hd * HEAD_DIM, (hd + 1) * HEAD_DIM)
        qh = qkv[:, sl]
        q_ref[0, :, sl] = qh * lax.rsqrt(jnp.sum(qh * qh, axis=-1, keepdims=True) + EPS) * (HEAD_DIM ** -0.5)
        kh = qkv[:, DN_WIDTH + hd * HEAD_DIM:DN_WIDTH + (hd + 1) * HEAD_DIM]
        k_ref[0, :, sl] = kh * lax.rsqrt(jnp.sum(kh * kh, axis=-1, keepdims=True) + EPS)
    v_ref[0] = qkv[:, 2 * DN_WIDTH:3 * DN_WIDTH]

    gate_ref[0] = _silu(_dot(h, w_ref[:, 3 * DN_WIDTH:4 * DN_WIDTH]))

    ba = _dot(h, wba_ref[...])
    lane = lax.broadcasted_iota(jnp.int32, ba.shape, 1)
    g = -jnp.exp(alog_ref[...]) * _softplus(ba + dtb_ref[...])
    gb_ref[0] = jnp.where(lane < N_HEADS, jax.nn.sigmoid(ba), g)

    u0 = 4 * DN_WIDTH
    u = _gelu_tanh(_dot(h, w_ref[:, u0:u0 + SG_WIDTH]))
    vg = _gelu_tanh(_dot(h, w_ref[:, u0 + SG_WIDTH:u0 + 2 * SG_WIDTH]))
    ri = lax.broadcasted_iota(jnp.int32, (SG_BLOCK, SG_BLOCK), 0) // CHUNK
    ci = lax.broadcasted_iota(jnp.int32, (SG_BLOCK, SG_BLOCK), 1) // CHUNK
    for gi in range(SG_GROUPS):
        sl = slice(gi * SG_DIM, (gi + 1) * SG_DIM)
        vn = (_rms(vg[:, sl]) * sgn_ref[:, sl]).astype(BF16)
        wm = jnp.where(ci <= ri, sgw_ref[gi], 0.0).astype(BF16)
        for nb in range(tt // SG_BLOCK):
            rs = slice(nb * SG_BLOCK, (nb + 1) * SG_BLOCK)
            s = _dot(wm, vn[rs]) + sgb_ref[gi]
            osg_ref[0, rs, sl] = (u[rs, sl] * s).astype(BF16)


def _merge_masks(n):
    row = lax.broadcasted_iota(jnp.int32, (n, n), 0)
    col = lax.broadcasted_iota(jnp.int32, (n, n), 1)
    masks = []
    b = 1
    while b < n:
        masks.append((row // (2 * b) == col // (2 * b)) & ((row // b) % 2 == 1) & ((col // b) % 2 == 0))
        b *= 2
    return masks


def _unit_lower_inverses(lmats, masks, eye):
    xs = [eye - jnp.where(masks[0], lm, 0.0) for lm in lmats]
    for m in masks[1:]:
        xbs = [x.astype(BF16) for x in xs]
        ys = [_dot(jnp.where(m, lm, 0.0).astype(BF16), xb) for lm, xb in zip(lmats, xbs)]
        xs = [x - _dot(xb, y.astype(BF16)) for x, xb, y in zip(xs, xbs, ys)]
    return xs


def _deltanet_kernel(q_ref, k_ref, v_ref, gate_ref, gb_ref, og_ref, o_ref,
                     s_ref, kq_ref, val_ref, kt_ref, attn_ref, cd_ref, *, tt, group):
    t = pl.program_id(1)

    @pl.when(t == 0)
    def _():
        s_ref[...] = jnp.zeros(s_ref.shape, F32)

    c = CHUNK
    row = lax.broadcasted_iota(jnp.int32, (c, c), 0)
    col = lax.broadcasted_iota(jnp.int32, (c, c), 1)
    incl = row >= col
    strict = row > col
    tril = incl.astype(F32)
    eye = (row == col).astype(F32)
    masks = _merge_masks(c)
    heads = [slice(hd * HEAD_DIM, (hd + 1) * HEAD_DIM) for hd in range(N_HEADS)]

    def prep_body(gi, carry):
        cis = [gi * group + j for j in range(group)]
        rows = [pl.ds(pl.multiple_of(ci * c, c), c) for ci in cis]
        beta, gc = [], []
        for r in rows:
            gb = gb_ref[0, r, :]
            beta.append([jnp.broadcast_to(gb[:, hd:hd + 1], (c, HEAD_DIM)) for hd in range(N_HEADS)])
            g4 = jnp.concatenate([jnp.broadcast_to(gb[:, N_HEADS + hd:N_HEADS + hd + 1], (c, HEAD_DIM))
                                  for hd in range(N_HEADS)], axis=1)
            gc.append(_dot(tril, g4, HIGHEST))
        probs = [(j, hd) for j in range(group) for hd in range(N_HEADS)]
        kb = [k_ref[0, rows[j], heads[hd]] * beta[j][hd] for j, hd in probs]
        kq = [_dot_nt(jnp.concatenate([kb[p], q_ref[0, rows[j], heads[hd]]], axis=0).astype(BF16),
                      k_ref[0, rows[j], heads[hd]].astype(BF16)) for p, (j, hd) in enumerate(probs)]
        decay = []
        for j, hd in probs:
            gch = gc[j][:, heads[hd]]
            gc_cols = jnp.transpose(gch)[0:c, :]
            decay.append(jnp.exp(jnp.where(incl, gch[:, 0:c] - gc_cols, -jnp.inf)))
        lmats = [jnp.where(strict, kq[p][0:c] * decay[p], 0.0) for p in range(len(probs))]
        tinv = _unit_lower_inverses(lmats, masks, eye)
        for p, (j, hd) in enumerate(probs):
            r, sl, ci = rows[j], heads[hd], cis[j]
            gch = gc[j][:, sl]
            g_last = gch[c - 1:c, :]
            e_gc = jnp.exp(gch)
            rhs = jnp.concatenate([v_ref[0, r, sl] * beta[j][hd], kb[p] * e_gc], axis=1)
            sol = _dot(tinv[p].astype(BF16), rhs.astype(BF16))
            val_ref[r, sl] = sol[:, 0:HEAD_DIM]
            kq_ref[ci, 0:c, sl] = sol[:, HEAD_DIM:2 * HEAD_DIM].astype(BF16)
            kq_ref[ci, c:2 * c, sl] = (q_ref[0, r, sl] * e_gc).astype(BF16)
            kt_ref[r, sl] = (k_ref[0, r, sl] * jnp.exp(g_last - gch)).astype(BF16)
            attn_ref[ci, :, hd * c:(hd + 1) * c] = (kq[p][c:2 * c] * decay[p]).astype(BF16)
            cd_ref[ci, :, sl] = jnp.broadcast_to(jnp.exp(g_last), (8, HEAD_DIM))
        return carry

    lax.fori_loop(0, tt // (c * group), prep_body, 0)

    def scan_body(ci, carry):
        r = pl.ds(pl.multiple_of(ci * c, c), c)
        s = [s_ref[hd] for hd in range(N_HEADS)]
        ps = [_dot(kq_ref[ci, :, heads[hd]], s[hd].astype(BF16)) for hd in range(N_HEADS)]
        v_new = [(val_ref[r, heads[hd]] - ps[hd][0:c]).astype(BF16) for hd in range(N_HEADS)]
        o = [ps[hd][c:2 * c] + _dot(attn_ref[ci, :, hd * c:(hd + 1) * c], v_new[hd]) for hd in range(N_HEADS)]
        for hd in range(N_HEADS):
            s_ref[hd] = s[hd] * cd_ref[ci, 0:1, heads[hd]] + _dot_tn(kt_ref[r, heads[hd]], v_new[hd])
        for hd in range(N_HEADS):
            sl = heads[hd]
            o_ref[0, r, sl] = (_rms(o[hd]) * og_ref[...] * gate_ref[0, r, sl]).astype(BF16)
        return carry

    lax.fori_loop(0, tt // c, scan_body, 0)


def _ffn_kernel(x_ref, odn_ref, osg_ref, wo_ref, fg_ref, wup_ref, cw_ref, cb_ref, wdn_ref, fin_ref,
                out_ref, up_ref, act_ref, *, tt, d_ff, col_tile, final):
    t = pl.program_id(1)
    x1 = (x_ref[0] + _dot(odn_ref[0], wo_ref[0:DN_WIDTH, :])
          + _dot(osg_ref[0], wo_ref[DN_WIDTH:DN_WIDTH + SG_WIDTH, :]))
    h = (_rms(x1) * fg_ref[...]).astype(BF16)

    @pl.when(t == 0)
    def _():
        up_ref[0:HALO, :] = jnp.zeros((HALO, 2 * d_ff), F32)

    @pl.when(t > 0)
    def _():
        up_ref[0:HALO, :] = up_ref[tt:tt + HALO, :]

    for j in range(2 * d_ff // col_tile):
        cs = slice(j * col_tile, (j + 1) * col_tile)
        up_ref[HALO:HALO + tt, cs] = _dot(h, wup_ref[:, cs])

    def conv(cs):
        acc = up_ref[HALO - 2:HALO - 2 + tt, cs] * cw_ref[0:1, cs] + cb_ref[:, cs]
        for j in range(1, FFN_CONV):
            acc = acc + up_ref[HALO - 2 + j:HALO - 2 + j + tt, cs] * cw_ref[j:j + 1, cs]
        return acc

    for j in range(d_ff // col_tile):
        gs = slice(j * col_tile, (j + 1) * col_tile)
        vs = slice(d_ff + j * col_tile, d_ff + (j + 1) * col_tile)
        act_ref[:, gs] = (_silu(conv(gs)) * conv(vs)).astype(BF16)

    x2 = x1 + _dot(act_ref[...], wdn_ref[...])
    out_ref[0] = _rms(x2) * fin_ref[...] if final else x2


def _const_spec(shape):
    nd = len(shape)
    return pl.BlockSpec(shape, lambda b, t: (0,) * nd, pipeline_mode=pl.Buffered(1))


def _tile_spec(tt, width):
    return pl.BlockSpec((1, tt, width), lambda b, t: (b, t, 0))


def kernel(x, attn_norm_g, w_in, dn_conv_w, dn_a_log, dn_dt_bias, dn_out_norm_g, sg_norm_g, sg_w, sg_b,
           w_out, ffn_norm_g, w_up, ffn_conv_w, ffn_conv_b, w_down, final_norm_g):
    bsz, seq, d_model = x.shape
    depth = w_in.shape[0]
    d_ff = w_down.shape[1]
    n_main = 4 * DN_WIDTH + 2 * SG_WIDTH
    assert w_in.shape[2] == n_main + 2 * N_HEADS
    params = pltpu.CompilerParams(dimension_semantics=("arbitrary", "arbitrary"),
                                  vmem_limit_bytes=V7X_VMEM_LIMIT_BYTES)

    tt1, tt2, tt3 = 256, 512, 256
    col_tile = 128 * math.gcd(d_ff // 128, 11)
    assert seq % tt1 == 0 and seq % tt2 == 0 and seq % tt3 == 0 and d_ff % col_tile == 0

    for l in range(depth):
        w_main = w_in[l, :, :n_main].astype(BF16)
        w_ba = jnp.pad(w_in[l, :, n_main:], ((0, 0), (0, 128 - 2 * N_HEADS))).astype(BF16)
        lane_pad = (N_HEADS, 128 - 2 * N_HEADS)
        alog_row = jnp.pad(dn_a_log[l].astype(F32), lane_pad).reshape(1, 128)
        dtb_row = jnp.pad(dn_dt_bias[l].astype(F32), lane_pad).reshape(1, 128)
        sgb_rep = jnp.broadcast_to(sg_b[l].astype(F32)[:, :, None], (SG_GROUPS, SG_BLOCK, SG_DIM))

        q, k, v, gate, gb, o_sg = pl.pallas_call(
            functools.partial(_inproj_kernel, tt=tt1),
            grid=(bsz, seq // tt1),
            in_specs=[
                _tile_spec(tt1, d_model),
                _const_spec((1, d_model)),
                _const_spec((d_model, n_main)),
                _const_spec((d_model, 128)),
                _const_spec((CONV_K, 3 * DN_WIDTH)),
                _const_spec((1, 128)),
                _const_spec((1, 128)),
                _const_spec((1, SG_WIDTH)),
                _const_spec((SG_GROUPS, SG_BLOCK, SG_BLOCK)),
                _const_spec((SG_GROUPS, SG_BLOCK, SG_DIM)),
            ],
            out_specs=[
                _tile_spec(tt1, DN_WIDTH), _tile_spec(tt1, DN_WIDTH), _tile_spec(tt1, DN_WIDTH),
                _tile_spec(tt1, DN_WIDTH), _tile_spec(tt1, 128), _tile_spec(tt1, SG_WIDTH),
            ],
            out_shape=[
                jax.ShapeDtypeStruct((bsz, seq, DN_WIDTH), F32),
                jax.ShapeDtypeStruct((bsz, seq, DN_WIDTH), F32),
                jax.ShapeDtypeStruct((bsz, seq, DN_WIDTH), F32),
                jax.ShapeDtypeStruct((bsz, seq, DN_WIDTH), F32),
                jax.ShapeDtypeStruct((bsz, seq, 128), F32),
                jax.ShapeDtypeStruct((bsz, seq, SG_WIDTH), BF16),
            ],
            scratch_shapes=[pltpu.VMEM((tt1 + HALO, 3 * DN_WIDTH), F32)],
            compiler_params=params,
            name="inproj",
        )(x, attn_norm_g[l].reshape(1, d_model), w_main, w_ba, dn_conv_w[l], alog_row, dtb_row,
          sg_norm_g[l].reshape(1, SG_WIDTH), sg_w[l], sgb_rep)

        o_dn = pl.pallas_call(
            functools.partial(_deltanet_kernel, tt=tt2, group=8),
            grid=(bsz, seq // tt2),
            in_specs=[
                _tile_spec(tt2, DN_WIDTH), _tile_spec(tt2, DN_WIDTH), _tile_spec(tt2, DN_WIDTH),
                _tile_spec(tt2, DN_WIDTH), _tile_spec(tt2, 128),
                _const_spec((1, HEAD_DIM)),
            ],
            out_specs=_tile_spec(tt2, DN_WIDTH),
            out_shape=jax.ShapeDtypeStruct((bsz, seq, DN_WIDTH), BF16),
            scratch_shapes=[pltpu.VMEM((N_HEADS, HEAD_DIM, HEAD_DIM), F32),
                            pltpu.VMEM((tt2 // CHUNK, 2 * CHUNK, DN_WIDTH), BF16),
                            pltpu.VMEM((tt2, DN_WIDTH), F32),
                            pltpu.VMEM((tt2, DN_WIDTH), BF16),
                            pltpu.VMEM((tt2 // CHUNK, CHUNK, N_HEADS * CHUNK), BF16),
                            pltpu.VMEM((tt2 // CHUNK, 8, DN_WIDTH), F32)],
            compiler_params=params,
            name="deltanet",
        )(q, k, v, gate, gb, dn_out_norm_g[l].reshape(1, HEAD_DIM))

        x = pl.pallas_call(
            functools.partial(_ffn_kernel, tt=tt3, d_ff=d_ff, col_tile=col_tile, final=(l == depth - 1)),
            grid=(bsz, seq // tt3),
            in_specs=[
                _tile_spec(tt3, d_model), _tile_spec(tt3, DN_WIDTH), _tile_spec(tt3, SG_WIDTH),
                _const_spec((DN_WIDTH + SG_WIDTH, d_model)),
                _const_spec((1, d_model)),
                _const_spec((d_model, 2 * d_ff)),
                _const_spec((FFN_CONV, 2 * d_ff)),
                _const_spec((1, 2 * d_ff)),
                _const_spec((d_ff, d_model)),
                _const_spec((1, d_model)),
            ],
            out_specs=_tile_spec(tt3, d_model),
            out_shape=jax.ShapeDtypeStruct((bsz, seq, d_model), F32),
            scratch_shapes=[pltpu.VMEM((tt3 + HALO, 2 * d_ff), F32),
                            pltpu.VMEM((tt3, d_ff), BF16)],
            compiler_params=params,
            name="ffn",
        )(x, o_dn, o_sg, w_out[l].astype(BF16), ffn_norm_g[l].reshape(1, d_model),
          w_up[l].astype(BF16), ffn_conv_w[l], ffn_conv_b[l].reshape(1, 2 * d_ff),
          w_down[l].astype(BF16),
          final_norm_g.reshape(1, d_model))
    return x
```

```python
import functools
import math

import jax
import jax.numpy as jnp
from jax import lax
from jax.experimental import pallas as pl
from jax.experimental.pallas import tpu as pltpu

F32 = jnp.float32
BF16 = jnp.bfloat16

EPS = 1e-6
CHUNK = 64
HEAD_DIM = 128
N_HEADS = 4
DN_WIDTH = N_HEADS * HEAD_DIM
SG_GROUPS = 4
SG_DIM = 128
SG_BLOCK = 128
SG_WIDTH = SG_GROUPS * SG_DIM
CONV_K = 4
FFN_CONV = 3
HALO = 8
LANES = 128

V7X_VMEM_LIMIT_BYTES = 56 * 1024 * 1024

HIGHEST = lax.Precision.HIGHEST


def _silu(x):
    return x * jax.nn.sigmoid(x)


def _gelu_tanh(x):
    c = math.sqrt(2.0 / math.pi)
    return x * (0.5 * (1.0 + jnp.tanh(c * (x + 0.044715 * (x * x * x)))))


def _softplus(x):
    return jnp.maximum(x, 0.0) + jnp.log1p(jnp.exp(-jnp.abs(x)))


def _rms(x):
    return x * lax.rsqrt(jnp.mean(x * x, axis=-1, keepdims=True) + EPS)


def _dot(a, b, precision=None):
    return jnp.dot(a, b, preferred_element_type=F32, precision=precision)


def _dot_nt(a, b):
    return lax.dot_general(a, b, (((1,), (1,)), ((), ())), preferred_element_type=F32)


def _dot_tn(a, b):
    return lax.dot_general(a, b, (((0,), (0,)), ((), ())), preferred_element_type=F32)


def _inproj_kernel(x_ref, ng_ref, w_ref, wba_ref, cw_ref, alog_ref, dtb_ref, sgn_ref,
                   sgw_ref, sgb_ref,
                   q_ref, k_ref, v_ref, gate_ref, gb_ref, osg_ref,
                   pre_ref, *, tt):
    t = pl.program_id(1)
    h = (_rms(x_ref[0]) * ng_ref[...]).astype(BF16)

    half = tt // 2

    @pl.when(t == 0)
    def _():
        pre_ref[:, 0:HALO, :] = jnp.zeros((pre_ref.shape[0], HALO, LANES), F32)

    @pl.when(t > 0)
    def _():
        pre_ref[:, 0:HALO, :] = pre_ref[:, tt:tt + HALO, :]

    pre = _dot(h, w_ref[:, 0:3 * DN_WIDTH])
    for s in range(3 * N_HEADS):
        pre_ref[s, HALO:HALO + tt, :] = pre[:, s * LANES:(s + 1) * LANES]

    for s in range(3 * N_HEADS):
        cs = slice(s * LANES, (s + 1) * LANES)
        kind, hd = divmod(s, N_HEADS)
        for parity in range(2):
            acc = pre_ref[s, pl.ds(HALO + parity - (CONV_K - 1), half, stride=2), :] * cw_ref[0:1, cs]
            for j in range(1, CONV_K):
                start = HALO + parity - (CONV_K - 1) + j
                acc = acc + pre_ref[s, pl.ds(start, half, stride=2), :] * cw_ref[j:j + 1, cs]
            y = _silu(acc)
            rows = pl.ds(parity, half, stride=2)
            if kind == 0:
                q_ref[0, hd, rows, :] = y * lax.rsqrt(jnp.sum(y * y, axis=-1, keepdims=True) + EPS) * (HEAD_DIM ** -0.5)
            elif kind == 1:
                k_ref[0, hd, rows, :] = y * lax.rsqrt(jnp.sum(y * y, axis=-1, keepdims=True) + EPS)
            else:
                v_ref[0, hd, rows, :] = y

    gate_ref[0] = _silu(_dot(h, w_ref[:, 3 * DN_WIDTH:4 * DN_WIDTH]))

    ba = _dot(h, wba_ref[...])
    lane = lax.broadcasted_iota(jnp.int32, ba.shape, 1)
    g = -jnp.exp(alog_ref[...]) * _softplus(ba + dtb_ref[...])
    gb_ref[0] = jnp.where(lane < N_HEADS, jax.nn.sigmoid(ba), g)

    u0 = 4 * DN_WIDTH
    u = _gelu_tanh(_dot(h, w_ref[:, u0:u0 + SG_WIDTH]))
    vg = _gelu_tanh(_dot(h, w_ref[:, u0 + SG_WIDTH:u0 + 2 * SG_WIDTH]))
    ri = lax.broadcasted_iota(jnp.int32, (SG_BLOCK, SG_BLOCK), 0) // CHUNK
    ci = lax.broadcasted_iota(jnp.int32, (SG_BLOCK, SG_BLOCK), 1) // CHUNK
    for gi in range(SG_GROUPS):
        sl = slice(gi * SG_DIM, (gi + 1) * SG_DIM)
        vn = (_rms(vg[:, sl]) * sgn_ref[:, sl]).astype(BF16)
        wm = jnp.where(ci <= ri, sgw_ref[gi], 0.0).astype(BF16)
        for nb in range(tt // SG_BLOCK):
            rs = slice(nb * SG_BLOCK, (nb + 1) * SG_BLOCK)
            s = _dot(wm, vn[rs]) + sgb_ref[gi]
            osg_ref[0, rs, sl] = (u[rs, sl] * s).astype(BF16)


def _merge_masks(n):
    row = lax.broadcasted_iota(jnp.int32, (n, n), 0)
    col = lax.broadcasted_iota(jnp.int32, (n, n), 1)
    masks = []
    b = 1
    while b < n:
        masks.append((row // (2 * b) == col // (2 * b)) & ((row // b) % 2 == 1) & ((col // b) % 2 == 0))
        b *= 2
    return masks


def _unit_lower_inverses(lmats, masks, eye):
    xs = [eye - jnp.where(masks[0], lm, 0.0) for lm in lmats]
    for m in masks[1:]:
        xbs = [x.astype(BF16) for x in xs]
        ys = [_dot(jnp.where(m, lm, 0.0).astype(BF16), xb) for lm, xb in zip(lmats, xbs)]
        xs = [x - _dot(xb, y.astype(BF16)) for x, xb, y in zip(xs, xbs, ys)]
    return xs


def _deltanet_kernel(q_ref, k_ref, v_ref, gate_ref, gb_ref, og_ref, o_ref,
                     s_ref, kq_ref, val_ref, kt_ref, attn_ref, cd_ref, *, tt, group):
    t = pl.program_id(1)

    @pl.when(t == 0)
    def _():
        s_ref[...] = jnp.zeros(s_ref.shape, F32)

    c = CHUNK
    row = lax.broadcasted_iota(jnp.int32, (c, c), 0)
    col = lax.broadcasted_iota(jnp.int32, (c, c), 1)
    incl = row >= col
    strict = row > col
    tril = incl.astype(F32)
    eye = (row == col).astype(F32)
    masks = _merge_masks(c)
    heads = [slice(hd * HEAD_DIM, (hd + 1) * HEAD_DIM) for hd in range(N_HEADS)]

    def prep_body(gi, carry):
        cis = [gi * group + j for j in range(group)]
        rows = [pl.ds(pl.multiple_of(ci * c, c), c) for ci in cis]
        beta, gc = [], []
        for r in rows:
            gb = gb_ref[0, r, :]
            beta.append([jnp.broadcast_to(gb[:, hd:hd + 1], (c, HEAD_DIM)) for hd in range(N_HEADS)])
            g4 = jnp.concatenate([jnp.broadcast_to(gb[:, N_HEADS + hd:N_HEADS + hd + 1], (c, HEAD_DIM))
                                  for hd in range(N_HEADS)], axis=1)
            gc.append(_dot(tril, g4, HIGHEST))
        probs = [(j, hd) for j in range(group) for hd in range(N_HEADS)]
        kb = [k_ref[0, hd, rows[j], :] * beta[j][hd] for j, hd in probs]
        kq = [_dot_nt(jnp.concatenate([kb[p], q_ref[0, hd, rows[j], :]], axis=0).astype(BF16),
                      k_ref[0, hd, rows[j], :].astype(BF16)) for p, (j, hd) in enumerate(probs)]
        decay = []
        for j, hd in probs:
            gch = gc[j][:, heads[hd]]
            gc_cols = jnp.transpose(gch)[0:c, :]
            decay.append(jnp.exp(jnp.where(incl, gch[:, 0:c] - gc_cols, -jnp.inf)))
        lmats = [jnp.where(strict, kq[p][0:c] * decay[p], 0.0) for p in range(len(probs))]
        tinv = _unit_lower_inverses(lmats, masks, eye)
        for p, (j, hd) in enumerate(probs):
            r, sl, ci = rows[j], heads[hd], cis[j]
            gch = gc[j][:, sl]
            g_last = gch[c - 1:c, :]
            e_gc = jnp.exp(gch)
            rhs = jnp.concatenate([v_ref[0, hd, r, :] * beta[j][hd], kb[p] * e_gc], axis=1)
            sol = _dot(tinv[p].astype(BF16), rhs.astype(BF16))
            val_ref[r, sl] = sol[:, 0:HEAD_DIM]
            kq_ref[ci, 0:c, sl] = sol[:, HEAD_DIM:2 * HEAD_DIM].astype(BF16)
            kq_ref[ci, c:2 * c, sl] = (q_ref[0, hd, r, :] * e_gc).astype(BF16)
            kt_ref[r, sl] = (k_ref[0, hd, r, :] * jnp.exp(g_last - gch)).astype(BF16)
            attn_ref[ci, :, hd * c:(hd + 1) * c] = (kq[p][c:2 * c] * decay[p]).astype(BF16)
            cd_ref[ci, :, sl] = jnp.broadcast_to(jnp.exp(g_last), (8, HEAD_DIM))
        return carry

    lax.fori_loop(0, tt // (c * group), prep_body, 0)

    def scan_body(ci, carry):
        r = pl.ds(pl.multiple_of(ci * c, c), c)
        s = [s_ref[hd] for hd in range(N_HEADS)]
        ps = [_dot(kq_ref[ci, :, heads[hd]], s[hd].astype(BF16)) for hd in range(N_HEADS)]
        v_new = [(val_ref[r, heads[hd]] - ps[hd][0:c]).astype(BF16) for hd in range(N_HEADS)]
        o = [ps[hd][c:2 * c] + _dot(attn_ref[ci, :, hd * c:(hd + 1) * c], v_new[hd]) for hd in range(N_HEADS)]
        for hd in range(N_HEADS):
            s_ref[hd] = s[hd] * cd_ref[ci, 0:1, heads[hd]] + _dot_tn(kt_ref[r, heads[hd]], v_new[hd])
        for hd in range(N_HEADS):
            sl = heads[hd]
            o_ref[0, r, sl] = (_rms(o[hd]) * og_ref[...] * gate_ref[0, r, sl]).astype(BF16)
        return carry

    lax.fori_loop(0, tt // c, scan_body, 0)


def _ffn_kernel(x_ref, odn_ref, osg_ref, wo_ref, fg_ref, wup_ref, cw_ref, cb_ref, wdn_ref, fin_ref,
                out_ref, up_ref, act_ref, dn_ref, *, tt, d_ff, col_tile, final):
    t = pl.program_id(1)
    d_model = x_ref.shape[2]
    half = tt // 2
    x1 = (x_ref[0] + _dot(odn_ref[0], wo_ref[0:DN_WIDTH, :])
          + _dot(osg_ref[0], wo_ref[DN_WIDTH:DN_WIDTH + SG_WIDTH, :]))
    h = (_rms(x1) * fg_ref[...]).astype(BF16)

    @pl.when(t == 0)
    def _():
        up_ref[:, 0:HALO, :] = jnp.zeros((up_ref.shape[0], HALO, LANES), F32)

    @pl.when(t > 0)
    def _():
        up_ref[:, 0:HALO, :] = up_ref[:, tt:tt + HALO, :]

    for j in range(2 * d_ff // col_tile):
        res = _dot(h, wup_ref[:, j * col_tile:(j + 1) * col_tile])
        for b in range(col_tile // LANES):
            up_ref[j * (col_tile // LANES) + b, HALO:HALO + tt, :] = res[:, b * LANES:(b + 1) * LANES]

    def conv(slab):
        cs = slice(slab * LANES, (slab + 1) * LANES)
        parts = []
        for parity in range(2):
            acc = cb_ref[:, cs]
            for k in range(FFN_CONV):
                start = HALO + parity - (FFN_CONV - 1) + k
                acc = acc + up_ref[slab, pl.ds(start, half, stride=2), :] * cw_ref[k:k + 1, cs]
            parts.append(acc)
        return jnp.concatenate(parts, axis=0)

    n_slab = d_ff // LANES
    for s in range(n_slab):
        act_ref[:, s * LANES:(s + 1) * LANES] = (_silu(conv(s)) * conv(n_slab + s)).astype(BF16)

    down = _dot(act_ref[...], wdn_ref[...])
    for b in range(d_model // LANES):
        dn_ref[b, pl.ds(0, half, stride=2), :] = down[0:half, b * LANES:(b + 1) * LANES]
        dn_ref[b, pl.ds(1, half, stride=2), :] = down[half:tt, b * LANES:(b + 1) * LANES]
    x2 = x1 + jnp.concatenate([dn_ref[b] for b in range(d_model // LANES)], axis=1)
    out_ref[0] = _rms(x2) * fin_ref[...] if final else x2


def _const_spec(shape):
    nd = len(shape)
    return pl.BlockSpec(shape, lambda b, t: (0,) * nd, pipeline_mode=pl.Buffered(1))


def _head_spec(tt):
    return pl.BlockSpec((1, N_HEADS, tt, HEAD_DIM), lambda b, t: (b, 0, t, 0))


def _tile_spec(tt, width):
    return pl.BlockSpec((1, tt, width), lambda b, t: (b, t, 0))


def kernel(x, attn_norm_g, w_in, dn_conv_w, dn_a_log, dn_dt_bias, dn_out_norm_g, sg_norm_g, sg_w, sg_b,
           w_out, ffn_norm_g, w_up, ffn_conv_w, ffn_conv_b, w_down, final_norm_g):
    bsz, seq, d_model = x.shape
    depth = w_in.shape[0]
    d_ff = w_down.shape[1]
    n_main = 4 * DN_WIDTH + 2 * SG_WIDTH
    assert w_in.shape[2] == n_main + 2 * N_HEADS
    params = pltpu.CompilerParams(dimension_semantics=("arbitrary", "arbitrary"),
                                  vmem_limit_bytes=V7X_VMEM_LIMIT_BYTES)

    tt1, tt2, tt3 = 256, 512, 256
    col_tile = 128 * math.gcd(d_ff // 128, 11)
    assert seq % tt1 == 0 and seq % tt2 == 0 and seq % tt3 == 0 and d_ff % col_tile == 0

    for l in range(depth):
        w_main = w_in[l, :, :n_main].astype(BF16)
        w_ba = jnp.pad(w_in[l, :, n_main:], ((0, 0), (0, 128 - 2 * N_HEADS))).astype(BF16)
        lane_pad = (N_HEADS, 128 - 2 * N_HEADS)
        alog_row = jnp.pad(dn_a_log[l].astype(F32), lane_pad).reshape(1, 128)
        dtb_row = jnp.pad(dn_dt_bias[l].astype(F32), lane_pad).reshape(1, 128)
        sgb_rep = jnp.broadcast_to(sg_b[l].astype(F32)[:, :, None], (SG_GROUPS, SG_BLOCK, SG_DIM))

        q, k, v, gate, gb, o_sg = pl.pallas_call(
            functools.partial(_inproj_kernel, tt=tt1),
            grid=(bsz, seq // tt1),
            in_specs=[
                _tile_spec(tt1, d_model),
                _const_spec((1, d_model)),
                _const_spec((d_model, n_main)),
                _const_spec((d_model, 128)),
                _const_spec((CONV_K, 3 * DN_WIDTH)),
                _const_spec((1, 128)),
                _const_spec((1, 128)),
                _const_spec((1, SG_WIDTH)),
                _const_spec((SG_GROUPS, SG_BLOCK, SG_BLOCK)),
                _const_spec((SG_GROUPS, SG_BLOCK, SG_DIM)),
            ],
            out_specs=[
                _head_spec(tt1), _head_spec(tt1), _head_spec(tt1),
                _tile_spec(tt1, DN_WIDTH), _tile_spec(tt1, 128), _tile_spec(tt1, SG_WIDTH),
            ],
            out_shape=[
                jax.ShapeDtypeStruct((bsz, N_HEADS, seq, HEAD_DIM), F32),
                jax.ShapeDtypeStruct((bsz, N_HEADS, seq, HEAD_DIM), F32),
                jax.ShapeDtypeStruct((bsz, N_HEADS, seq, HEAD_DIM), F32),
                jax.ShapeDtypeStruct((bsz, seq, DN_WIDTH), F32),
                jax.ShapeDtypeStruct((bsz, seq, 128), F32),
                jax.ShapeDtypeStruct((bsz, seq, SG_WIDTH), BF16),
            ],
            scratch_shapes=[pltpu.VMEM((3 * N_HEADS, tt1 + HALO, LANES), F32)],
            compiler_params=params,
            name="inproj",
        )(x, attn_norm_g[l].reshape(1, d_model), w_main, w_ba, dn_conv_w[l], alog_row, dtb_row,
          sg_norm_g[l].reshape(1, SG_WIDTH), sg_w[l], sgb_rep)

        o_dn = pl.pallas_call(
            functools.partial(_deltanet_kernel, tt=tt2, group=8),
            grid=(bsz, seq // tt2),
            in_specs=[
                _head_spec(tt2), _head_spec(tt2), _head_spec(tt2),
                _tile_spec(tt2, DN_WIDTH), _tile_spec(tt2, 128),
                _const_spec((1, HEAD_DIM)),
            ],
            out_specs=_tile_spec(tt2, DN_WIDTH),
            out_shape=jax.ShapeDtypeStruct((bsz, seq, DN_WIDTH), BF16),
            scratch_shapes=[pltpu.VMEM((N_HEADS, HEAD_DIM, HEAD_DIM), F32),
                            pltpu.VMEM((tt2 // CHUNK, 2 * CHUNK, DN_WIDTH), BF16),
                            pltpu.VMEM((tt2, DN_WIDTH), F32),
                            pltpu.VMEM((tt2, DN_WIDTH), BF16),
                            pltpu.VMEM((tt2 // CHUNK, CHUNK, N_HEADS * CHUNK), BF16),
                            pltpu.VMEM((tt2 // CHUNK, 8, DN_WIDTH), F32)],
            compiler_params=params,
            name="deltanet",
        )(q, k, v, gate, gb, dn_out_norm_g[l].reshape(1, HEAD_DIM))

        x = pl.pallas_call(
            functools.partial(_ffn_kernel, tt=tt3, d_ff=d_ff, col_tile=col_tile, final=(l == depth - 1)),
            grid=(bsz, seq // tt3),
            in_specs=[
                _tile_spec(tt3, d_model), _tile_spec(tt3, DN_WIDTH), _tile_spec(tt3, SG_WIDTH),
                _const_spec((DN_WIDTH + SG_WIDTH, d_model)),
                _const_spec((1, d_model)),
                _const_spec((d_model, 2 * d_ff)),
                _const_spec((FFN_CONV, 2 * d_ff)),
                _const_spec((1, 2 * d_ff)),
                _const_spec((d_ff, d_model)),
                _const_spec((1, d_model)),
            ],
            out_specs=_tile_spec(tt3, d_model),
            out_shape=jax.ShapeDtypeStruct((bsz, seq, d_model), F32),
            scratch_shapes=[pltpu.VMEM((2 * d_ff // LANES, tt3 + HALO, LANES), F32),
                            pltpu.VMEM((tt3, d_ff), BF16),
                            pltpu.VMEM((d_model // LANES, tt3, LANES), F32)],
            compiler_params=params,
            name="ffn",
        )(x, o_dn, o_sg, w_out[l].astype(BF16), ffn_norm_g[l].reshape(1, d_model),
          w_up[l].astype(BF16), ffn_conv_w[l], ffn_conv_b[l].reshape(1, 2 * d_ff),
          w_down[l].astype(BF16),
          final_norm_g.reshape(1, d_model))
    return x
```

```python
import functools
import math

import jax
import jax.numpy as jnp
from jax import lax
from jax.experimental import pallas as pl
from jax.experimental.pallas import tpu as pltpu

F32 = jnp.float32
BF16 = jnp.bfloat16

EPS = 1e-6
CHUNK = 64
HEAD_DIM = 128
N_HEADS = 4
DN_WIDTH = N_HEADS * HEAD_DIM
SG_GROUPS = 4
SG_DIM = 128
SG_BLOCK = 128
SG_WIDTH = SG_GROUPS * SG_DIM
CONV_K = 4
FFN_CONV = 3
HALO = 8
LANES = 128

V7X_VMEM_LIMIT_BYTES = 56 * 1024 * 1024

HIGHEST = lax.Precision.HIGHEST


def _silu(x):
    return x * jax.nn.sigmoid(x)


def _gelu_tanh(x):
    c = math.sqrt(2.0 / math.pi)
    return x * (0.5 * (1.0 + jnp.tanh(c * (x + 0.044715 * (x * x * x)))))


def _softplus(x):
    return jnp.maximum(x, 0.0) + jnp.log1p(jnp.exp(-jnp.abs(x)))


def _rms(x):
    return x * lax.rsqrt(jnp.mean(x * x, axis=-1, keepdims=True) + EPS)


def _dot(a, b, precision=None):
    return jnp.dot(a, b, preferred_element_type=F32, precision=precision)


def _dot_nt(a, b):
    return lax.dot_general(a, b, (((1,), (1,)), ((), ())), preferred_element_type=F32)


def _dot_tn(a, b):
    return lax.dot_general(a, b, (((0,), (0,)), ((), ())), preferred_element_type=F32)


def _inproj_kernel(x_ref, ng_ref, w_ref, wba_ref, cw_ref, alog_ref, dtb_ref, sgn_ref,
                   sgw_ref, sgb_ref,
                   q_ref, k_ref, v_ref, gate_ref, gb_ref, osg_ref,
                   pre_ref, *, tt):
    t = pl.program_id(1)
    h = (_rms(x_ref[0]) * ng_ref[...]).astype(BF16)

    half = tt // 2

    @pl.when(t == 0)
    def _():
        pre_ref[:, 0:HALO, :] = jnp.zeros((pre_ref.shape[0], HALO, LANES), F32)

    @pl.when(t > 0)
    def _():
        pre_ref[:, 0:HALO, :] = pre_ref[:, tt:tt + HALO, :]

    pre = _dot(h, w_ref[:, 0:3 * DN_WIDTH])
    for s in range(3 * N_HEADS):
        pre_ref[s, HALO:HALO + tt, :] = pre[:, s * LANES:(s + 1) * LANES]

    for s in range(3 * N_HEADS):
        cs = slice(s * LANES, (s + 1) * LANES)
        kind, hd = divmod(s, N_HEADS)
        for parity in range(2):
            acc = pre_ref[s, pl.ds(HALO + parity - (CONV_K - 1), half, stride=2), :] * cw_ref[0:1, cs]
            for j in range(1, CONV_K):
                start = HALO + parity - (CONV_K - 1) + j
                acc = acc + pre_ref[s, pl.ds(start, half, stride=2), :] * cw_ref[j:j + 1, cs]
            y = _silu(acc)
            rows = pl.ds(parity, half, stride=2)
            if kind == 0:
                q_ref[0, hd, rows, :] = y * lax.rsqrt(jnp.sum(y * y, axis=-1, keepdims=True) + EPS) * (HEAD_DIM ** -0.5)
            elif kind == 1:
                k_ref[0, hd, rows, :] = y * lax.rsqrt(jnp.sum(y * y, axis=-1, keepdims=True) + EPS)
            else:
                v_ref[0, hd, rows, :] = y

    gate_ref[0] = _silu(_dot(h, w_ref[:, 3 * DN_WIDTH:4 * DN_WIDTH]))

    ba = _dot(h, wba_ref[...])
    lane = lax.broadcasted_iota(jnp.int32, ba.shape, 1)
    g = -jnp.exp(alog_ref[...]) * _softplus(ba + dtb_ref[...])
    gb_ref[0] = jnp.where(lane < N_HEADS, jax.nn.sigmoid(ba), g)

    u0 = 4 * DN_WIDTH
    u = _gelu_tanh(_dot(h, w_ref[:, u0:u0 + SG_WIDTH]))
    vg = _gelu_tanh(_dot(h, w_ref[:, u0 + SG_WIDTH:u0 + 2 * SG_WIDTH]))
    ri = lax.broadcasted_iota(jnp.int32, (SG_BLOCK, SG_BLOCK), 0) // CHUNK
    ci = lax.broadcasted_iota(jnp.int32, (SG_BLOCK, SG_BLOCK), 1) // CHUNK
    for gi in range(SG_GROUPS):
        sl = slice(gi * SG_DIM, (gi + 1) * SG_DIM)
        vn = (_rms(vg[:, sl]) * sgn_ref[:, sl]).astype(BF16)
        wm = jnp.where(ci <= ri, sgw_ref[gi], 0.0).astype(BF16)
        for nb in range(tt // SG_BLOCK):
            rs = slice(nb * SG_BLOCK, (nb + 1) * SG_BLOCK)
            s = _dot(wm, vn[rs]) + sgb_ref[gi]
            osg_ref[0, rs, sl] = (u[rs, sl] * s).astype(BF16)


def _merge_masks(n):
    row = lax.broadcasted_iota(jnp.int32, (n, n), 0)
    col = lax.broadcasted_iota(jnp.int32, (n, n), 1)
    masks = []
    b = 1
    while b < n:
        masks.append((row // (2 * b) == col // (2 * b)) & ((row // b) % 2 == 1) & ((col // b) % 2 == 0))
        b *= 2
    return masks


def _unit_lower_inverses(lmats, masks, eye):
    xs = [eye - jnp.where(masks[0], lm, 0.0) for lm in lmats]
    for m in masks[1:]:
        xbs = [x.astype(BF16) for x in xs]
        ys = [_dot(jnp.where(m, lm, 0.0).astype(BF16), xb) for lm, xb in zip(lmats, xbs)]
        xs = [x - _dot(xb, y.astype(BF16)) for x, xb, y in zip(xs, xbs, ys)]
    return xs


def _deltanet_kernel(q_ref, k_ref, v_ref, gate_ref, gb_ref, og_ref, o_ref,
                     s_ref, kq_ref, val_ref, kt_ref, attn_ref, cd_ref, *, tt, group):
    t = pl.program_id(1)

    @pl.when(t == 0)
    def _():
        s_ref[...] = jnp.zeros(s_ref.shape, F32)

    c = CHUNK
    row = lax.broadcasted_iota(jnp.int32, (c, c), 0)
    col = lax.broadcasted_iota(jnp.int32, (c, c), 1)
    incl = row >= col
    strict = row > col
    tril = incl.astype(F32)
    eye = (row == col).astype(F32)
    masks = _merge_masks(c)
    heads = [slice(hd * HEAD_DIM, (hd + 1) * HEAD_DIM) for hd in range(N_HEADS)]

    def prep_body(gi, carry):
        cis = [gi * group + j for j in range(group)]
        rows = [pl.ds(pl.multiple_of(ci * c, c), c) for ci in cis]
        beta, gc = [], []
        for r in rows:
            gb = gb_ref[0, r, :]
            beta.append([jnp.broadcast_to(gb[:, hd:hd + 1], (c, HEAD_DIM)) for hd in range(N_HEADS)])
            g4 = jnp.concatenate([jnp.broadcast_to(gb[:, N_HEADS + hd:N_HEADS + hd + 1], (c, HEAD_DIM))
                                  for hd in range(N_HEADS)], axis=1)
            gc.append(_dot(tril, g4, HIGHEST))
        probs = [(j, hd) for j in range(group) for hd in range(N_HEADS)]
        kb = [k_ref[0, hd, rows[j], :] * beta[j][hd] for j, hd in probs]
        kq = [_dot_nt(jnp.concatenate([kb[p], q_ref[0, hd, rows[j], :]], axis=0).astype(BF16),
                      k_ref[0, hd, rows[j], :].astype(BF16)) for p, (j, hd) in enumerate(probs)]
        decay = []
        for j, hd in probs:
            gch = gc[j][:, heads[hd]]
            gc_cols = jnp.transpose(gch)[0:c, :]
            decay.append(jnp.exp(jnp.where(incl, gch[:, 0:c] - gc_cols, -jnp.inf)))
        lmats = [jnp.where(strict, kq[p][0:c] * decay[p], 0.0) for p in range(len(probs))]
        tinv = _unit_lower_inverses(lmats, masks, eye)
        for p, (j, hd) in enumerate(probs):
            r, sl, ci = rows[j], heads[hd], cis[j]
            gch = gc[j][:, sl]
            g_last = gch[c - 1:c, :]
            e_gc = jnp.exp(gch)
            rhs = jnp.concatenate([v_ref[0, hd, r, :] * beta[j][hd], kb[p] * e_gc], axis=1)
            sol = _dot(tinv[p].astype(BF16), rhs.astype(BF16))
            val_ref[r, sl] = sol[:, 0:HEAD_DIM]
            kq_ref[ci, 0:c, sl] = sol[:, HEAD_DIM:2 * HEAD_DIM].astype(BF16)
            kq_ref[ci, c:2 * c, sl] = (q_ref[0, hd, r, :] * e_gc).astype(BF16)
            kt_ref[r, sl] = (k_ref[0, hd, r, :] * jnp.exp(g_last - gch)).astype(BF16)
            attn_ref[ci, :, hd * c:(hd + 1) * c] = (kq[p][c:2 * c] * decay[p]).astype(BF16)
            cd_ref[ci, :, sl] = jnp.broadcast_to(jnp.exp(g_last), (8, HEAD_DIM))
        return carry

    lax.fori_loop(0, tt // (c * group), prep_body, 0)

    def scan_body(ci, carry):
        r = pl.ds(pl.multiple_of(ci * c, c), c)
        s = [s_ref[hd] for hd in range(N_HEADS)]
        ps = [_dot(kq_ref[ci, :, heads[hd]], s[hd].astype(BF16)) for hd in range(N_HEADS)]
        v_new = [(val_ref[r, heads[hd]] - ps[hd][0:c]).astype(BF16) for hd in range(N_HEADS)]
        o = [ps[hd][c:2 * c] + _dot(attn_ref[ci, :, hd * c:(hd + 1) * c], v_new[hd]) for hd in range(N_HEADS)]
        for hd in range(N_HEADS):
            s_ref[hd] = s[hd] * cd_ref[ci, 0:1, heads[hd]] + _dot_tn(kt_ref[r, heads[hd]], v_new[hd])
        for hd in range(N_HEADS):
            sl = heads[hd]
            o_ref[0, r, sl] = (_rms(o[hd]) * og_ref[...] * gate_ref[0, r, sl]).astype(BF16)
        return carry

    lax.fori_loop(0, tt // c, scan_body, 0)


def _ffn_kernel(x_ref, odn_ref, osg_ref, wo_ref, fg_ref, wup_ref, cw_ref, cb_ref, wdn_ref, fin_ref,
                out_ref, up_ref, act_ref, dn_ref, *, tt, d_ff, col_tile, final):
    t = pl.program_id(1)
    d_model = x_ref.shape[2]
    half = tt // 2
    x1 = (x_ref[0] + _dot(odn_ref[0], wo_ref[0:DN_WIDTH, :])
          + _dot(osg_ref[0], wo_ref[DN_WIDTH:DN_WIDTH + SG_WIDTH, :]))
    h = (_rms(x1) * fg_ref[...]).astype(BF16)

    @pl.when(t == 0)
    def _():
        up_ref[:, 0:HALO, :] = jnp.zeros((up_ref.shape[0], HALO, LANES), F32)

    @pl.when(t > 0)
    def _():
        up_ref[:, 0:HALO, :] = up_ref[:, tt:tt + HALO, :]

    for j in range(2 * d_ff // col_tile):
        res = _dot(h, wup_ref[:, j * col_tile:(j + 1) * col_tile])
        for b in range(col_tile // LANES):
            up_ref[j * (col_tile // LANES) + b, HALO:HALO + tt, :] = res[:, b * LANES:(b + 1) * LANES]

    def conv(slab):
        cs = slice(slab * LANES, (slab + 1) * LANES)
        parts = []
        for parity in range(2):
            acc = cb_ref[:, cs]
            for k in range(FFN_CONV):
                start = HALO + parity - (FFN_CONV - 1) + k
                acc = acc + up_ref[slab, pl.ds(start, half, stride=2), :] * cw_ref[k:k + 1, cs]
            parts.append(acc)
        return jnp.concatenate(parts, axis=0)

    n_slab = d_ff // LANES
    for s in range(n_slab):
        act_ref[:, s * LANES:(s + 1) * LANES] = (_silu(conv(s)) * conv(n_slab + s)).astype(BF16)

    down = _dot(act_ref[...], wdn_ref[...])
    for b in range(d_model // LANES):
        dn_ref[b, pl.ds(0, half, stride=2), :] = down[0:half, b * LANES:(b + 1) * LANES]
        dn_ref[b, pl.ds(1, half, stride=2), :] = down[half:tt, b * LANES:(b + 1) * LANES]
    x2 = x1 + jnp.concatenate([dn_ref[b] for b in range(d_model // LANES)], axis=1)
    out_ref[0] = _rms(x2) * fin_ref[...] if final else x2


def _const_spec(shape):
    nd = len(shape)
    return pl.BlockSpec(shape, lambda b, t: (0,) * nd, pipeline_mode=pl.Buffered(1))


def _head_spec(tt):
    return pl.BlockSpec((1, N_HEADS, tt, HEAD_DIM), lambda b, t: (b, 0, t, 0))


def _tile_spec(tt, width):
    return pl.BlockSpec((1, tt, width), lambda b, t: (b, t, 0))


def kernel(x, attn_norm_g, w_in, dn_conv_w, dn_a_log, dn_dt_bias, dn_out_norm_g, sg_norm_g, sg_w, sg_b,
           w_out, ffn_norm_g, w_up, ffn_conv_w, ffn_conv_b, w_down, final_norm_g):
    bsz, seq, d_model = x.shape
    depth = w_in.shape[0]
    d_ff = w_down.shape[1]
    n_main = 4 * DN_WIDTH + 2 * SG_WIDTH
    assert w_in.shape[2] == n_main + 2 * N_HEADS
    params = pltpu.CompilerParams(dimension_semantics=("arbitrary", "arbitrary"),
                                  vmem_limit_bytes=V7X_VMEM_LIMIT_BYTES)

    tt1, tt2, tt3 = 512, 512, 512
    col_tile = 128 * math.gcd(d_ff // 128, 11)
    assert seq % tt1 == 0 and seq % tt2 == 0 and seq % tt3 == 0 and d_ff % col_tile == 0

    for l in range(depth):
        w_main = w_in[l, :, :n_main].astype(BF16)
        w_ba = jnp.pad(w_in[l, :, n_main:], ((0, 0), (0, 128 - 2 * N_HEADS))).astype(BF16)
        lane_pad = (N_HEADS, 128 - 2 * N_HEADS)
        alog_row = jnp.pad(dn_a_log[l].astype(F32), lane_pad).reshape(1, 128)
        dtb_row = jnp.pad(dn_dt_bias[l].astype(F32), lane_pad).reshape(1, 128)
        sgb_rep = jnp.broadcast_to(sg_b[l].astype(F32)[:, :, None], (SG_GROUPS, SG_BLOCK, SG_DIM))

        q, k, v, gate, gb, o_sg = pl.pallas_call(
            functools.partial(_inproj_kernel, tt=tt1),
            grid=(bsz, seq // tt1),
            in_specs=[
                _tile_spec(tt1, d_model),
                _const_spec((1, d_model)),
                _const_spec((d_model, n_main)),
                _const_spec((d_model, 128)),
                _const_spec((CONV_K, 3 * DN_WIDTH)),
                _const_spec((1, 128)),
                _const_spec((1, 128)),
                _const_spec((1, SG_WIDTH)),
                _const_spec((SG_GROUPS, SG_BLOCK, SG_BLOCK)),
                _const_spec((SG_GROUPS, SG_BLOCK, SG_DIM)),
            ],
            out_specs=[
                _head_spec(tt1), _head_spec(tt1), _head_spec(tt1),
                _tile_spec(tt1, DN_WIDTH), _tile_spec(tt1, 128), _tile_spec(tt1, SG_WIDTH),
            ],
            out_shape=[
                jax.ShapeDtypeStruct((bsz, N_HEADS, seq, HEAD_DIM), F32),
                jax.ShapeDtypeStruct((bsz, N_HEADS, seq, HEAD_DIM), F32),
                jax.ShapeDtypeStruct((bsz, N_HEADS, seq, HEAD_DIM), F32),
                jax.ShapeDtypeStruct((bsz, seq, DN_WIDTH), F32),
                jax.ShapeDtypeStruct((bsz, seq, 128), F32),
                jax.ShapeDtypeStruct((bsz, seq, SG_WIDTH), BF16),
            ],
            scratch_shapes=[pltpu.VMEM((3 * N_HEADS, tt1 + HALO, LANES), F32)],
            compiler_params=params,
            name="inproj",
        )(x, attn_norm_g[l].reshape(1, d_model), w_main, w_ba, dn_conv_w[l], alog_row, dtb_row,
          sg_norm_g[l].reshape(1, SG_WIDTH), sg_w[l], sgb_rep)

        o_dn = pl.pallas_call(
            functools.partial(_deltanet_kernel, tt=tt2, group=8),
            grid=(bsz, seq // tt2),
            in_specs=[
                _head_spec(tt2), _head_spec(tt2), _head_spec(tt2),
                _tile_spec(tt2, DN_WIDTH), _tile_spec(tt2, 128),
                _const_spec((1, HEAD_DIM)),
            ],
            out_specs=_tile_spec(tt2, DN_WIDTH),
            out_shape=jax.ShapeDtypeStruct((bsz, seq, DN_WIDTH), BF16),
            scratch_shapes=[pltpu.VMEM((N_HEADS, HEAD_DIM, HEAD_DIM), F32),
                            pltpu.VMEM((tt2 // CHUNK, 2 * CHUNK, DN_WIDTH), BF16),
                            pltpu.VMEM((tt2, DN_WIDTH), F32),
                            pltpu.VMEM((tt2, DN_WIDTH), BF16),
                            pltpu.VMEM((tt2 // CHUNK, CHUNK, N_HEADS * CHUNK), BF16),
                            pltpu.VMEM((tt2 // CHUNK, 8, DN_WIDTH), F32)],
            compiler_params=params,
            name="deltanet",
        )(q, k, v, gate, gb, dn_out_norm_g[l].reshape(1, HEAD_DIM))

        x = pl.pallas_call(
            functools.partial(_ffn_kernel, tt=tt3, d_ff=d_ff, col_tile=col_tile, final=(l == depth - 1)),
            grid=(bsz, seq // tt3),
            in_specs=[
                _tile_spec(tt3, d_model), _tile_spec(tt3, DN_WIDTH), _tile_spec(tt3, SG_WIDTH),
                _const_spec((DN_WIDTH + SG_WIDTH, d_model)),
                _const_spec((1, d_model)),
                _const_spec((d_model, 2 * d_ff)),
                _const_spec((FFN_CONV, 2 * d_ff)),
                _const_spec((1, 2 * d_ff)),
                _const_spec((d_ff, d_model)),
                _const_spec((1, d_model)),
            ],
            out_specs=_tile_spec(tt3, d_model),
            out_shape=jax.ShapeDtypeStruct((bsz, seq, d_model), F32),
            scratch_shapes=[pltpu.VMEM((2 * d_ff // LANES, tt3 + HALO, LANES), F32),
                            pltpu.VMEM((tt3, d_ff), BF16),
                            pltpu.VMEM((d_model // LANES, tt3, LANES), F32)],
            compiler_params=params,
            name="ffn",
        )(x, o_dn, o_sg, w_out[l].astype(BF16), ffn_norm_g[l].reshape(1, d_model),
          w_up[l].astype(BF16), ffn_conv_w[l], ffn_conv_b[l].reshape(1, 2 * d_ff),
          w_down[l].astype(BF16),
          final_norm_g.reshape(1, d_model))
    return x
```

```python
import functools
import math

import jax
import jax.numpy as jnp
from jax import lax
from jax.experimental import pallas as pl
from jax.experimental.pallas import tpu as pltpu

F32 = jnp.float32
BF16 = jnp.bfloat16

EPS = 1e-6
CHUNK = 64
HEAD_DIM = 128
N_HEADS = 4
DN_WIDTH = N_HEADS * HEAD_DIM
SG_GROUPS = 4
SG_DIM = 128
SG_BLOCK = 128
SG_WIDTH = SG_GROUPS * SG_DIM
CONV_K = 4
FFN_CONV = 3
HALO = 8
LANES = 128

V7X_VMEM_LIMIT_BYTES = 56 * 1024 * 1024

HIGHEST = lax.Precision.HIGHEST


def _silu(x):
    return x * jax.nn.sigmoid(x)


def _gelu_tanh(x):
    c = math.sqrt(2.0 / math.pi)
    return x * (0.5 * (1.0 + jnp.tanh(c * (x + 0.044715 * (x * x * x)))))


def _softplus(x):
    return jnp.maximum(x, 0.0) + jnp.log1p(jnp.exp(-jnp.abs(x)))


def _rms(x):
    return x * lax.rsqrt(jnp.mean(x * x, axis=-1, keepdims=True) + EPS)


def _dot(a, b, precision=None):
    return jnp.dot(a, b, preferred_element_type=F32, precision=precision)


def _dot_nt(a, b):
    return lax.dot_general(a, b, (((1,), (1,)), ((), ())), preferred_element_type=F32)


def _dot_tn(a, b):
    return lax.dot_general(a, b, (((0,), (0,)), ((), ())), preferred_element_type=F32)


def _inproj_kernel(x_ref, ng_ref, w_ref, wba_ref, cw_ref, alog_ref, dtb_ref, sgn_ref,
                   sgw_ref, sgb_ref,
                   q_ref, k_ref, v_ref, gate_ref, gb_ref, osg_ref,
                   pre_ref, *, tt):
    t = pl.program_id(1)
    h = (_rms(x_ref[0]) * ng_ref[...]).astype(BF16)

    half = tt // 2

    @pl.when(t == 0)
    def _():
        pre_ref[:, 0:HALO, :] = jnp.zeros((pre_ref.shape[0], HALO, LANES), F32)

    @pl.when(t > 0)
    def _():
        pre_ref[:, 0:HALO, :] = pre_ref[:, tt:tt + HALO, :]

    pre = _dot(h, w_ref[:, 0:3 * DN_WIDTH])
    for s in range(3 * N_HEADS):
        pre_ref[s, HALO:HALO + tt, :] = pre[:, s * LANES:(s + 1) * LANES]

    for s in range(3 * N_HEADS):
        cs = slice(s * LANES, (s + 1) * LANES)
        kind, hd = divmod(s, N_HEADS)
        for parity in range(2):
            acc = pre_ref[s, pl.ds(HALO + parity - (CONV_K - 1), half, stride=2), :] * cw_ref[0:1, cs]
            for j in range(1, CONV_K):
                start = HALO + parity - (CONV_K - 1) + j
                acc = acc + pre_ref[s, pl.ds(start, half, stride=2), :] * cw_ref[j:j + 1, cs]
            y = _silu(acc)
            rows = pl.ds(parity, half, stride=2)
            if kind == 0:
                q_ref[0, hd, rows, :] = y * lax.rsqrt(jnp.sum(y * y, axis=-1, keepdims=True) + EPS) * (HEAD_DIM ** -0.5)
            elif kind == 1:
                k_ref[0, hd, rows, :] = y * lax.rsqrt(jnp.sum(y * y, axis=-1, keepdims=True) + EPS)
            else:
                v_ref[0, hd, rows, :] = y

    gate_ref[0] = _silu(_dot(h, w_ref[:, 3 * DN_WIDTH:4 * DN_WIDTH]))

    ba = _dot(h, wba_ref[...])
    lane = lax.broadcasted_iota(jnp.int32, ba.shape, 1)
    g = -jnp.exp(alog_ref[...]) * _softplus(ba + dtb_ref[...])
    gb_ref[0] = jnp.where(lane < N_HEADS, jax.nn.sigmoid(ba), g)

    u0 = 4 * DN_WIDTH
    u = _gelu_tanh(_dot(h, w_ref[:, u0:u0 + SG_WIDTH]))
    vg = _gelu_tanh(_dot(h, w_ref[:, u0 + SG_WIDTH:u0 + 2 * SG_WIDTH]))
    ri = lax.broadcasted_iota(jnp.int32, (SG_BLOCK, SG_BLOCK), 0) // CHUNK
    ci = lax.broadcasted_iota(jnp.int32, (SG_BLOCK, SG_BLOCK), 1) // CHUNK
    for gi in range(SG_GROUPS):
        sl = slice(gi * SG_DIM, (gi + 1) * SG_DIM)
        vn = (_rms(vg[:, sl]) * sgn_ref[:, sl]).astype(BF16)
        wm = jnp.where(ci <= ri, sgw_ref[gi], 0.0).astype(BF16)
        for nb in range(tt // SG_BLOCK):
            rs = slice(nb * SG_BLOCK, (nb + 1) * SG_BLOCK)
            s = _dot(wm, vn[rs]) + sgb_ref[gi]
            osg_ref[0, rs, sl] = (u[rs, sl] * s).astype(BF16)


def _merge_masks(n):
    row = lax.broadcasted_iota(jnp.int32, (n, n), 0)
    col = lax.broadcasted_iota(jnp.int32, (n, n), 1)
    masks = []
    b = 1
    while b < n:
        masks.append((row // (2 * b) == col // (2 * b)) & ((row // b) % 2 == 1) & ((col // b) % 2 == 0))
        b *= 2
    return masks


def _unit_lower_inverses(lmats, masks, eye):
    xs = [eye - jnp.where(masks[0], lm, 0.0) for lm in lmats]
    for m in masks[1:]:
        xbs = [x.astype(BF16) for x in xs]
        ys = [_dot(jnp.where(m, lm, 0.0).astype(BF16), xb) for lm, xb in zip(lmats, xbs)]
        xs = [x - _dot(xb, y.astype(BF16)) for x, xb, y in zip(xs, xbs, ys)]
    return xs


def _deltanet_kernel(q_ref, k_ref, v_ref, gate_ref, gb_ref, og_ref, o_ref,
                     s_ref, kq_ref, val_ref, kt_ref, attn_ref, cd_ref, *, nb, tt, group):
    t = pl.program_id(0)

    @pl.when(t == 0)
    def _():
        s_ref[...] = jnp.zeros(s_ref.shape, F32)

    c = CHUNK
    n_chunk = tt // c
    row = lax.broadcasted_iota(jnp.int32, (c, c), 0)
    col = lax.broadcasted_iota(jnp.int32, (c, c), 1)
    incl = row >= col
    strict = row > col
    tril = incl.astype(F32)
    eye = (row == col).astype(F32)
    masks = _merge_masks(c)
    heads = [slice(hd * HEAD_DIM, (hd + 1) * HEAD_DIM) for hd in range(N_HEADS)]

    def prep_body(gi, carry):
        pairs = [gi * group + j for j in range(group)]
        seqs = [p // n_chunk for p in pairs]
        rows = [pl.ds(pl.multiple_of((p % n_chunk) * c, c), c) for p in pairs]
        beta, gc = [], []
        for bi, r in zip(seqs, rows):
            gb = gb_ref[bi, r, :]
            beta.append([jnp.broadcast_to(gb[:, hd:hd + 1], (c, HEAD_DIM)) for hd in range(N_HEADS)])
            g4 = jnp.concatenate([jnp.broadcast_to(gb[:, N_HEADS + hd:N_HEADS + hd + 1], (c, HEAD_DIM))
                                  for hd in range(N_HEADS)], axis=1)
            gc.append(_dot(tril, g4, HIGHEST))
        probs = [(j, hd) for j in range(group) for hd in range(N_HEADS)]
        kb = [k_ref[seqs[j], hd, rows[j], :] * beta[j][hd] for j, hd in probs]
        kq = [_dot_nt(jnp.concatenate([kb[p], q_ref[seqs[j], hd, rows[j], :]], axis=0).astype(BF16),
                      k_ref[seqs[j], hd, rows[j], :].astype(BF16)) for p, (j, hd) in enumerate(probs)]
        decay = []
        for j, hd in probs:
            gch = gc[j][:, heads[hd]]
            gc_cols = jnp.transpose(gch)[0:c, :]
            decay.append(jnp.exp(jnp.where(incl, gch[:, 0:c] - gc_cols, -jnp.inf)))
        lmats = [jnp.where(strict, kq[p][0:c] * decay[p], 0.0) for p in range(len(probs))]
        tinv = _unit_lower_inverses(lmats, masks, eye)
        for p, (j, hd) in enumerate(probs):
            bi, r, sl, f = seqs[j], rows[j], heads[hd], pairs[j]
            gch = gc[j][:, sl]
            g_last = gch[c - 1:c, :]
            e_gc = jnp.exp(gch)
            rhs = jnp.concatenate([v_ref[bi, hd, r, :] * beta[j][hd], kb[p] * e_gc], axis=1)
            sol = _dot(tinv[p].astype(BF16), rhs.astype(BF16))
            val_ref[f, :, sl] = sol[:, 0:HEAD_DIM]
            kq_ref[f, 0:c, sl] = sol[:, HEAD_DIM:2 * HEAD_DIM].astype(BF16)
            kq_ref[f, c:2 * c, sl] = (q_ref[bi, hd, r, :] * e_gc).astype(BF16)
            kt_ref[f, :, sl] = (k_ref[bi, hd, r, :] * jnp.exp(g_last - gch)).astype(BF16)
            attn_ref[f, :, hd * c:(hd + 1) * c] = (kq[p][c:2 * c] * decay[p]).astype(BF16)
            cd_ref[f, :, sl] = jnp.broadcast_to(jnp.exp(g_last), (8, HEAD_DIM))
        return carry

    lax.fori_loop(0, nb * n_chunk // group, prep_body, 0)

    chains = [(bi, hd) for bi in range(nb) for hd in range(N_HEADS)]

    def scan_body(ci, carry):
        r = pl.ds(pl.multiple_of(ci * c, c), c)
        f = [bi * n_chunk + ci for bi in range(nb)]
        s = [s_ref[bi * N_HEADS + hd] for bi, hd in chains]
        ps = [_dot(kq_ref[f[bi], :, heads[hd]], s[n].astype(BF16)) for n, (bi, hd) in enumerate(chains)]
        v_new = [(val_ref[f[bi], :, heads[hd]] - ps[n][0:c]).astype(BF16) for n, (bi, hd) in enumerate(chains)]
        o = [ps[n][c:2 * c] + _dot(attn_ref[f[bi], :, hd * c:(hd + 1) * c], v_new[n])
             for n, (bi, hd) in enumerate(chains)]
        for n, (bi, hd) in enumerate(chains):
            s_ref[bi * N_HEADS + hd] = (s[n] * cd_ref[f[bi], 0:1, heads[hd]]
                                        + _dot_tn(kt_ref[f[bi], :, heads[hd]], v_new[n]))
        for n, (bi, hd) in enumerate(chains):
            sl = heads[hd]
            o_ref[bi, r, sl] = (_rms(o[n]) * og_ref[...] * gate_ref[bi, r, sl]).astype(BF16)
        return carry

    lax.fori_loop(0, n_chunk, scan_body, 0)


def _ffn_kernel(x_ref, odn_ref, osg_ref, wo_ref, fg_ref, wup_ref, cw_ref, cb_ref, wdn_ref, fin_ref,
                out_ref, up_ref, act_ref, dn_ref, *, tt, d_ff, col_tile, final):
    t = pl.program_id(1)
    d_model = x_ref.shape[2]
    half = tt // 2
    x1 = (x_ref[0] + _dot(odn_ref[0], wo_ref[0:DN_WIDTH, :])
          + _dot(osg_ref[0], wo_ref[DN_WIDTH:DN_WIDTH + SG_WIDTH, :]))
    h = (_rms(x1) * fg_ref[...]).astype(BF16)

    @pl.when(t == 0)
    def _():
        up_ref[:, 0:HALO, :] = jnp.zeros((up_ref.shape[0], HALO, LANES), F32)

    @pl.when(t > 0)
    def _():
        up_ref[:, 0:HALO, :] = up_ref[:, tt:tt + HALO, :]

    for j in range(2 * d_ff // col_tile):
        res = _dot(h, wup_ref[:, j * col_tile:(j + 1) * col_tile])
        for b in range(col_tile // LANES):
            up_ref[j * (col_tile // LANES) + b, HALO:HALO + tt, :] = res[:, b * LANES:(b + 1) * LANES]

    def conv(slab):
        cs = slice(slab * LANES, (slab + 1) * LANES)
        parts = []
        for parity in range(2):
            acc = cb_ref[:, cs]
            for k in range(FFN_CONV):
                start = HALO + parity - (FFN_CONV - 1) + k
                acc = acc + up_ref[slab, pl.ds(start, half, stride=2), :] * cw_ref[k:k + 1, cs]
            parts.append(acc)
        return jnp.concatenate(parts, axis=0)

    n_slab = d_ff // LANES
    for s in range(n_slab):
        act_ref[:, s * LANES:(s + 1) * LANES] = (_silu(conv(s)) * conv(n_slab + s)).astype(BF16)

    down = _dot(act_ref[...], wdn_ref[...])
    for b in range(d_model // LANES):
        dn_ref[b, pl.ds(0, half, stride=2), :] = down[0:half, b * LANES:(b + 1) * LANES]
        dn_ref[b, pl.ds(1, half, stride=2), :] = down[half:tt, b * LANES:(b + 1) * LANES]
    x2 = x1 + jnp.concatenate([dn_ref[b] for b in range(d_model // LANES)], axis=1)
    out_ref[0] = _rms(x2) * fin_ref[...] if final else x2


def _const_spec(shape):
    nd = len(shape)
    return pl.BlockSpec(shape, lambda b, t: (0,) * nd, pipeline_mode=pl.Buffered(1))


def _head_spec(tt):
    return pl.BlockSpec((1, N_HEADS, tt, HEAD_DIM), lambda b, t: (b, 0, t, 0))


def _tile_spec(tt, width):
    return pl.BlockSpec((1, tt, width), lambda b, t: (b, t, 0))


def kernel(x, attn_norm_g, w_in, dn_conv_w, dn_a_log, dn_dt_bias, dn_out_norm_g, sg_norm_g, sg_w, sg_b,
           w_out, ffn_norm_g, w_up, ffn_conv_w, ffn_conv_b, w_down, final_norm_g):
    bsz, seq, d_model = x.shape
    depth = w_in.shape[0]
    d_ff = w_down.shape[1]
    n_main = 4 * DN_WIDTH + 2 * SG_WIDTH
    assert w_in.shape[2] == n_main + 2 * N_HEADS
    params = pltpu.CompilerParams(dimension_semantics=("arbitrary", "arbitrary"),
                                  vmem_limit_bytes=V7X_VMEM_LIMIT_BYTES)

    tt1, tt2, tt3 = 512, 256, 512
    col_tile = 128 * math.gcd(d_ff // 128, 11)
    assert seq % tt1 == 0 and seq % tt2 == 0 and seq % tt3 == 0 and d_ff % col_tile == 0

    for l in range(depth):
        w_main = w_in[l, :, :n_main].astype(BF16)
        w_ba = jnp.pad(w_in[l, :, n_main:], ((0, 0), (0, 128 - 2 * N_HEADS))).astype(BF16)
        lane_pad = (N_HEADS, 128 - 2 * N_HEADS)
        alog_row = jnp.pad(dn_a_log[l].astype(F32), lane_pad).reshape(1, 128)
        dtb_row = jnp.pad(dn_dt_bias[l].astype(F32), lane_pad).reshape(1, 128)
        sgb_rep = jnp.broadcast_to(sg_b[l].astype(F32)[:, :, None], (SG_GROUPS, SG_BLOCK, SG_DIM))

        q, k, v, gate, gb, o_sg = pl.pallas_call(
            functools.partial(_inproj_kernel, tt=tt1),
            grid=(bsz, seq // tt1),
            in_specs=[
                _tile_spec(tt1, d_model),
                _const_spec((1, d_model)),
                _const_spec((d_model, n_main)),
                _const_spec((d_model, 128)),
                _const_spec((CONV_K, 3 * DN_WIDTH)),
                _const_spec((1, 128)),
                _const_spec((1, 128)),
                _const_spec((1, SG_WIDTH)),
                _const_spec((SG_GROUPS, SG_BLOCK, SG_BLOCK)),
                _const_spec((SG_GROUPS, SG_BLOCK, SG_DIM)),
            ],
            out_specs=[
                _head_spec(tt1), _head_spec(tt1), _head_spec(tt1),
                _tile_spec(tt1, DN_WIDTH), _tile_spec(tt1, 128), _tile_spec(tt1, SG_WIDTH),
            ],
            out_shape=[
                jax.ShapeDtypeStruct((bsz, N_HEADS, seq, HEAD_DIM), F32),
                jax.ShapeDtypeStruct((bsz, N_HEADS, seq, HEAD_DIM), F32),
                jax.ShapeDtypeStruct((bsz, N_HEADS, seq, HEAD_DIM), F32),
                jax.ShapeDtypeStruct((bsz, seq, DN_WIDTH), F32),
                jax.ShapeDtypeStruct((bsz, seq, 128), F32),
                jax.ShapeDtypeStruct((bsz, seq, SG_WIDTH), BF16),
            ],
            scratch_shapes=[pltpu.VMEM((3 * N_HEADS, tt1 + HALO, LANES), F32)],
            compiler_params=params,
            name="inproj",
        )(x, attn_norm_g[l].reshape(1, d_model), w_main, w_ba, dn_conv_w[l], alog_row, dtb_row,
          sg_norm_g[l].reshape(1, SG_WIDTH), sg_w[l], sgb_rep)

        n_pair = bsz * (tt2 // CHUNK)
        o_dn = pl.pallas_call(
            functools.partial(_deltanet_kernel, nb=bsz, tt=tt2, group=math.gcd(n_pair, 8)),
            grid=(seq // tt2,),
            in_specs=[
                pl.BlockSpec((bsz, N_HEADS, tt2, HEAD_DIM), lambda t: (0, 0, t, 0)),
                pl.BlockSpec((bsz, N_HEADS, tt2, HEAD_DIM), lambda t: (0, 0, t, 0)),
                pl.BlockSpec((bsz, N_HEADS, tt2, HEAD_DIM), lambda t: (0, 0, t, 0)),
                pl.BlockSpec((bsz, tt2, DN_WIDTH), lambda t: (0, t, 0)),
                pl.BlockSpec((bsz, tt2, 128), lambda t: (0, t, 0)),
                pl.BlockSpec((1, HEAD_DIM), lambda t: (0, 0), pipeline_mode=pl.Buffered(1)),
            ],
            out_specs=pl.BlockSpec((bsz, tt2, DN_WIDTH), lambda t: (0, t, 0)),
            out_shape=jax.ShapeDtypeStruct((bsz, seq, DN_WIDTH), BF16),
            scratch_shapes=[pltpu.VMEM((bsz * N_HEADS, HEAD_DIM, HEAD_DIM), F32),
                            pltpu.VMEM((n_pair, 2 * CHUNK, DN_WIDTH), BF16),
                            pltpu.VMEM((n_pair, CHUNK, DN_WIDTH), F32),
                            pltpu.VMEM((n_pair, CHUNK, DN_WIDTH), BF16),
                            pltpu.VMEM((n_pair, CHUNK, N_HEADS * CHUNK), BF16),
                            pltpu.VMEM((n_pair, 8, DN_WIDTH), F32)],
            compiler_params=pltpu.CompilerParams(dimension_semantics=("arbitrary",),
                                                 vmem_limit_bytes=V7X_VMEM_LIMIT_BYTES),
            name="deltanet",
        )(q, k, v, gate, gb, dn_out_norm_g[l].reshape(1, HEAD_DIM))

        x = pl.pallas_call(
            functools.partial(_ffn_kernel, tt=tt3, d_ff=d_ff, col_tile=col_tile, final=(l == depth - 1)),
            grid=(bsz, seq // tt3),
            in_specs=[
                _tile_spec(tt3, d_model), _tile_spec(tt3, DN_WIDTH), _tile_spec(tt3, SG_WIDTH),
                _const_spec((DN_WIDTH + SG_WIDTH, d_model)),
                _const_spec((1, d_model)),
                _const_spec((d_model, 2 * d_ff)),
                _const_spec((FFN_CONV, 2 * d_ff)),
                _const_spec((1, 2 * d_ff)),
                _const_spec((d_ff, d_model)),
                _const_spec((1, d_model)),
            ],
            out_specs=_tile_spec(tt3, d_model),
            out_shape=jax.ShapeDtypeStruct((bsz, seq, d_model), F32),
            scratch_shapes=[pltpu.VMEM((2 * d_ff // LANES, tt3 + HALO, LANES), F32),
                            pltpu.VMEM((tt3, d_ff), BF16),
                            pltpu.VMEM((d_model // LANES, tt3, LANES), F32)],
            compiler_params=params,
            name="ffn",
        )(x, o_dn, o_sg, w_out[l].astype(BF16), ffn_norm_g[l].reshape(1, d_model),
          w_up[l].astype(BF16), ffn_conv_w[l], ffn_conv_b[l].reshape(1, 2 * d_ff),
          w_down[l].astype(BF16),
          final_norm_g.reshape(1, d_model))
    return x
```
